```python
import jax, jax.numpy as jnp
from jax import lax
import numpy as np

D_MODEL = 2048
BATCH = 2
SEQ = 4096
DEPTH = 4
DEC_BATCH = 32
DEC_SEQ = 8
PAST_LEN = 16384
PAGE_SIZE = 128

N_EVEN = (DEPTH + 1) // 2
N_ODD = DEPTH // 2
HEAD_DIM = 128
H_A = D_MODEL // (2 * HEAD_DIM)
KV_A = H_A // 2
G_A = H_A // KV_A
H_B = D_MODEL // (2 * HEAD_DIM)
KV_B = H_B // 2
G_B = H_B // KV_B
HD_C = 64
H_C = D_MODEL // HD_C
KV_C = H_C // 8
G_C = H_C // KV_C
WINDOW = 128
Q_BLOCK = 128
D_FF = 5632
ROPE_THETA = 10000.0
EPS = 1e-6
NEG_INF = -1e30
F32 = jnp.float32

QA_W = H_A * HEAD_DIM
KA_W = KV_A * HEAD_DIM
QB_W = H_B * HEAD_DIM
KB_W = KV_B * HEAD_DIM
EVEN_SPLITS = (QA_W, QA_W + KA_W, QA_W + 2 * KA_W, QA_W + 2 * KA_W + QB_W,
               QA_W + 2 * KA_W + QB_W + KB_W, QA_W + 2 * KA_W + QB_W + 2 * KB_W)
EVEN_COLS = QA_W + 2 * KA_W + QB_W + 2 * KB_W + H_B
EVEN_OUT = (H_A + H_B) * HEAD_DIM
ODD_COLS = (H_C + 2 * KV_C) * HD_C
ODD_OUT = H_C * HD_C

kernel_name = 'stickbreak_fox_swa_macaron_decode_step'


def rms_norm(x, g):
    x32 = x.astype(F32)
    y = x32 * lax.rsqrt(jnp.mean(x32 * x32, axis=-1, keepdims=True) + EPS)
    return (y * g.astype(F32)).astype(x.dtype)


def swiglu(h, w_in, w_out):
    gate, up = jnp.split(h @ w_in, 2, axis=-1)
    return (jax.nn.silu(gate) * up) @ w_out


def rope(x, pos):
    half = x.shape[-1] // 2
    inv_freq = ROPE_THETA ** (-jnp.arange(half, dtype=F32) / half)
    ang = pos.astype(F32)[:, None] * inv_freq[None, :]
    cos = jnp.cos(ang)[:, None, :]
    sin = jnp.sin(ang)[:, None, :]
    xf = x.astype(F32)
    x1, x2 = xf[..., :half], xf[..., half:]
    return jnp.concatenate([x1 * cos - x2 * sin, x2 * cos + x1 * sin], axis=-1).astype(x.dtype)


def heads_last(f, kv, g):
    b, s, _ = f.shape
    return f.reshape(b, s, kv, g).transpose(0, 2, 3, 1)


def gather_rows(cache, layer, page_table):
    rows = cache[layer, page_table]
    return rows.reshape(rows.shape[0], rows.shape[1] * rows.shape[2], *rows.shape[3:])


def sweep_query_blocks(fn, qpos, *per_query):
    t = qpos.shape[0]
    blk = min(Q_BLOCK, t)
    nb = t // blk

    def to_blocks(a):
        return jnp.moveaxis(a.reshape(a.shape[0], nb, blk, *a.shape[2:]), 1, 0)

    xs = (qpos.reshape(nb, blk),) + tuple(to_blocks(a) for a in per_query)
    out = lax.map(lambda args: fn(*args), xs)
    out = jnp.moveaxis(out, 0, 1)
    return out.reshape(out.shape[0], t, *out.shape[3:])


def segment_scores(q, ks):
    scale = q.shape[-1] ** -0.5
    z = jnp.concatenate([jnp.einsum('btkgd,bskd->bkgts', q, k, preferred_element_type=F32) for k in ks], axis=-1)
    return z * scale


def contract_segments(w, vs):
    out = None
    start = 0
    for v in vs:
        n = v.shape[1]
        part = jnp.einsum('bkgts,bskd->btkgd', w[..., start:start + n].astype(v.dtype), v)
        out = part if out is None else out + part
        start += n
    return out


def sb_attend(q, qpos, ks, vs, kposs):
    z = segment_scores(q, ks)
    kpos = jnp.concatenate(kposs)
    mask = kpos[None, :] < qpos[:, None]
    log_keep = jnp.where(mask, jax.nn.log_sigmoid(-z), 0.0)
    log_between = lax.cumsum(log_keep, axis=z.ndim - 1, reverse=True) - log_keep
    w = jnp.where(mask, jnp.exp(jax.nn.log_sigmoid(z) + log_between), 0.0)
    return contract_segments(w, vs)


def fox_attend(q, qpos, fq, ks, vs, kposs, fk_h):
    b, t, kv, g, _ = q.shape
    z = segment_scores(q, ks)
    kpos = jnp.concatenate(kposs)
    fq_h = fq.reshape(b, t, kv, g).transpose(0, 2, 3, 1)
    logits = z + fq_h[..., :, None] - fk_h[..., None, :]
    mask = kpos[None, :] <= qpos[:, None]
    w = jax.nn.softmax(jnp.where(mask, logits, NEG_INF), axis=-1)
    return contract_segments(w, vs)


def even_mixer(h, pos, past, w_in, b_f, w_out):
    b, t, _ = h.shape
    qa, ka, va, qb, kb, vb, fl = jnp.split(h @ w_in, EVEN_SPLITS, axis=-1)
    qa = qa.reshape(b, t, KV_A, G_A, HEAD_DIM)
    ka = ka.reshape(b, t, KV_A, HEAD_DIM)
    va = va.reshape(b, t, KV_A, HEAD_DIM)
    qb = qb.reshape(b, t, KV_B, G_B, HEAD_DIM)
    kb = kb.reshape(b, t, KV_B, HEAD_DIM)
    vb = vb.reshape(b, t, KV_B, HEAD_DIM)
    logf = jax.nn.log_sigmoid(fl.astype(F32) + b_f.astype(F32))
    if past is None:
        kposs = [pos]
        ka_s, va_s, kb_s, vb_s = [ka], [va], [kb], [vb]
        lf_all = logf
    else:
        pka, pva, pkb, pvb, plf, ppos = past
        kposs = [ppos, pos]
        ka_s, va_s, kb_s, vb_s = [pka, ka], [pva, va], [pkb, kb], [pvb, vb]
        lf_all = jnp.concatenate([plf.astype(F32), logf], axis=1)
    f_cum = jnp.cumsum(lf_all, axis=1)
    fk_h = heads_last(f_cum, KV_B, G_B)
    fq = f_cum[:, f_cum.shape[1] - t:]
    oa = sweep_query_blocks(lambda qp, qq: sb_attend(qq, qp, ka_s, va_s, kposs), pos, qa)
    ob = sweep_query_blocks(lambda qp, qq, ff: fox_attend(qq, qp, ff, kb_s, vb_s, kposs, fk_h), pos, qb, fq)
    y = jnp.concatenate([oa.reshape(b, t, QA_W), ob.reshape(b, t, QB_W)], axis=-1) @ w_out
    return y, (ka, va, kb, vb, logf.astype(h.dtype))


def odd_project(h, w_in, pos):
    b, t, _ = h.shape
    q, k, v = jnp.split(h @ w_in, [H_C * HD_C, (H_C + KV_C) * HD_C], axis=-1)
    q = rope(q.reshape(b, t, H_C, HD_C), pos).reshape(b, t, KV_C, G_C, HD_C)
    k = rope(k.reshape(b, t, KV_C, HD_C), pos)
    v = v.reshape(b, t, KV_C, HD_C)
    return q, k, v


def window_mask(qpos, kpos):
    d = qpos - kpos
    return (d >= 0) & (d < WINDOW)


def swa_core(q, k, v, mask, sinks):
    scale = q.shape[-1] ** -0.5
    logits = jnp.einsum('...tkgd,...skd->...kgts', q, k, preferred_element_type=F32) * scale
    logits = jnp.where(mask, logits, NEG_INF)
    sink = jnp.broadcast_to(sinks.astype(F32).reshape(KV_C, G_C, 1, 1), logits.shape[:-1] + (1,))
    p = jax.nn.softmax(jnp.concatenate([logits, sink], axis=-1), axis=-1)[..., :-1]
    return jnp.einsum('...kgts,...skd->...tkgd', p.astype(v.dtype), v)


def swa_prompt(h, pos, w_in, sinks, w_out):
    b, t, _ = h.shape
    nb = t // WINDOW
    q, k, v = odd_project(h, w_in, pos)
    qb = q.reshape(b, nb, WINDOW, KV_C, G_C, HD_C)

    def band(a):
        a = a.reshape(b, nb, WINDOW, KV_C, HD_C)
        prev = jnp.concatenate([jnp.zeros_like(a[:, :1]), a[:, :-1]], axis=1)
        return jnp.concatenate([prev, a], axis=2)

    qpos = pos.reshape(nb, WINDOW)
    kpos = jnp.concatenate([qpos - WINDOW, qpos], axis=1)
    mask = window_mask(qpos[:, :, None], kpos[:, None, :]) & (kpos >= 0)[:, None, :]
    o = swa_core(qb, band(k), band(v), mask[:, None, None], sinks)
    y = o.reshape(b, t, ODD_OUT) @ w_out
    return y, k[:, t - WINDOW:], v[:, t - WINDOW:]


def swa_sample(h, pos, buf_k, buf_v, w_in, sinks, w_out):
    b, t, _ = h.shape
    w = buf_k.shape[1]
    q, k, v = odd_project(h, w_in, pos)
    kk = jnp.concatenate([buf_k, k], axis=1)
    vv = jnp.concatenate([buf_v, v], axis=1)
    kpos = jnp.concatenate([pos[0] - w + jnp.arange(w, dtype=pos.dtype), pos])
    mask = window_mask(pos[:, None], kpos[None, :])
    o = swa_core(q, kk, vv, mask, sinks)
    y = o.reshape(b, t, ODD_OUT) @ w_out
    return y, kk[:, kk.shape[1] - w:], vv[:, vv.shape[1] - w:]


def _normal(key, shape, scale):
    return scale * jax.random.normal(key, shape, F32)


def setup_inputs(seed: int = 0) -> dict:
    key = jax.random.key(seed)
    ks = jax.random.split(key, 24)
    n_pages = PAST_LEN // PAGE_SIZE
    n_used = DEC_BATCH * n_pages
    n_phys = n_used + n_used // 4
    paged_a = (N_EVEN, n_phys, PAGE_SIZE, KV_A, HEAD_DIM)
    paged_b = (N_EVEN, n_phys, PAGE_SIZE, KV_B, HEAD_DIM)
    swa_buf = (N_ODD, DEC_BATCH, min(WINDOW, PAST_LEN), KV_C, HD_C)
    page_table = jax.random.permutation(ks[9], n_phys)[:n_used].reshape(DEC_BATCH, n_pages).astype(jnp.int32)
    return {
        'x_prompt': _normal(ks[0], (BATCH, SEQ, D_MODEL), 1.0),
        'x_sample': _normal(ks[1], (DEC_BATCH, DEC_SEQ, D_MODEL), 1.0),
        'cache_sb_k': _normal(ks[2], paged_a, 1.0),
        'cache_sb_v': _normal(ks[3], paged_a, 1.0),
        'cache_fox_k': _normal(ks[4], paged_b, 1.0),
        'cache_fox_v': _normal(ks[5], paged_b, 1.0),
        'cache_fox_logf': jax.nn.log_sigmoid(_normal(ks[6], (N_EVEN, n_phys, PAGE_SIZE, H_B), 1.0)),
        'cache_swa_k': _normal(ks[7], swa_buf, 1.0),
        'cache_swa_v': _normal(ks[8], swa_buf, 1.0),
        'page_table': page_table,
        'norm_ffn1': 1.0 + _normal(ks[10], (DEPTH, D_MODEL), 0.02),
        'norm_mix': 1.0 + _normal(ks[11], (DEPTH, D_MODEL), 0.02),
        'norm_ffn2': 1.0 + _normal(ks[12], (DEPTH, D_MODEL), 0.02),
        'norm_final': 1.0 + _normal(ks[13], (D_MODEL,), 0.02),
        'w_ffn_in': _normal(ks[14], (DEPTH, 2, D_MODEL, 2 * D_FF), D_MODEL ** -0.5),
        'w_ffn_out': _normal(ks[15], (DEPTH, 2, D_FF, D_MODEL), D_FF ** -0.5),
        'w_in_even': _normal(ks[16], (N_EVEN, D_MODEL, EVEN_COLS), D_MODEL ** -0.5),
        'b_forget': _normal(ks[17], (N_EVEN, H_B), 0.1),
        'w_out_even': _normal(ks[18], (N_EVEN, EVEN_OUT, D_MODEL), EVEN_OUT ** -0.5),
        'w_in_odd': _normal(ks[19], (N_ODD, D_MODEL, ODD_COLS), D_MODEL ** -0.5),
        'sinks': _normal(ks[20], (N_ODD, H_C), 0.5),
        'w_out_odd': _normal(ks[21], (N_ODD, ODD_OUT, D_MODEL), ODD_OUT ** -0.5),
    }


def reference(x_prompt, x_sample, cache_sb_k, cache_sb_v, cache_fox_k, cache_fox_v, cache_fox_logf,
              cache_swa_k, cache_swa_v, page_table, norm_ffn1, norm_mix, norm_ffn2, norm_final,
              w_ffn_in, w_ffn_out, w_in_even, b_forget, w_out_even, w_in_odd, sinks, w_out_odd):
    past_len = page_table.shape[1] * cache_sb_k.shape[2]
    pos_p = jnp.arange(x_prompt.shape[1], dtype=jnp.int32)
    pos_s = past_len + jnp.arange(x_sample.shape[1], dtype=jnp.int32)
    pos_past = jnp.arange(past_len, dtype=jnp.int32)
    even_p = ([], [], [], [], [])
    even_s = ([], [], [], [], [])
    odd_p = ([], [])
    odd_s = ([], [])
    xp, xs = x_prompt, x_sample
    for l in range(DEPTH):
        xp = xp + 0.5 * swiglu(rms_norm(xp, norm_ffn1[l]), w_ffn_in[l, 0], w_ffn_out[l, 0])
        xs = xs + 0.5 * swiglu(rms_norm(xs, norm_ffn1[l]), w_ffn_in[l, 0], w_ffn_out[l, 0])
        hp = rms_norm(xp, norm_mix[l])
        hs = rms_norm(xs, norm_mix[l])
        i = l // 2
        if l % 2 == 0:
            yp, new_p = even_mixer(hp, pos_p, None, w_in_even[i], b_forget[i], w_out_even[i])
            past = (gather_rows(cache_sb_k, i, page_table), gather_rows(cache_sb_v, i, page_table),
                    gather_rows(cache_fox_k, i, page_table), gather_rows(cache_fox_v, i, page_table),
                    gather_rows(cache_fox_logf, i, page_table), pos_past)
            ys, new_s = even_mixer(hs, pos_s, past, w_in_even[i], b_forget[i], w_out_even[i])
            for store, new in ((even_p, new_p), (even_s, new_s)):
                for lst, arr in zip(store, new):
                    lst.append(arr)
        else:
            yp, kp, vp = swa_prompt(hp, pos_p, w_in_odd[i], sinks[i], w_out_odd[i])
            ys, kn, vn = swa_sample(hs, pos_s, cache_swa_k[i], cache_swa_v[i], w_in_odd[i], sinks[i], w_out_odd[i])
            odd_p[0].append(kp)
            odd_p[1].append(vp)
            odd_s[0].append(kn)
            odd_s[1].append(vn)
        xp = xp + yp
        xs = xs + ys
        xp = xp + 0.5 * swiglu(rms_norm(xp, norm_ffn2[l]), w_ffn_in[l, 1], w_ffn_out[l, 1])
        xs = xs + 0.5 * swiglu(rms_norm(xs, norm_ffn2[l]), w_ffn_in[l, 1], w_ffn_out[l, 1])
    y_prompt = rms_norm(xp, norm_final)
    y_sample = rms_norm(xs, norm_final)
    sb_k_p, sb_v_p, fox_k_p, fox_v_p, fox_logf_p = [jnp.stack(a) for a in even_p]
    sb_k_s, sb_v_s, fox_k_s, fox_v_s, fox_logf_s = [jnp.stack(a) for a in even_s]
    swa_k_p, swa_v_p = [jnp.stack(a) for a in odd_p]
    swa_k_s, swa_v_s = [jnp.stack(a) for a in odd_s]
    return (y_prompt, y_sample, sb_k_p, sb_v_p, fox_k_p, fox_v_p, fox_logf_p, swa_k_p, swa_v_p,
            sb_k_s, sb_v_s, fox_k_s, fox_v_s, fox_logf_s, swa_k_s, swa_v_s)
```

```python
import functools

import jax
import jax.numpy as jnp
from jax import lax
from jax.experimental import pallas as pl
from jax.experimental.pallas import tpu as pltpu

F32 = jnp.float32
BF16 = jnp.bfloat16

HEAD_DIM = 128
HD_C = 64
WINDOW = 128
ROPE_THETA = 10000.0
EPS = 1e-6
NEG_INF = -1e30
LANES = 128
SUBLANES = 8
VMEM_LIMIT = 56 * 1024 * 1024


def _pick_tile(n, target, mult):
    best = None
    for t in range(mult, min(n, target) + 1, mult):
        if n % t == 0:
            best = t
    assert best is not None, (n, target, mult)
    return best


def _params(*sem):
    return pltpu.CompilerParams(dimension_semantics=sem, vmem_limit_bytes=VMEM_LIMIT)


def _dot(a, b):
    return jnp.dot(a, b, preferred_element_type=F32)


def _dot_nt(a, b):
    return lax.dot_general(a, b, (((1,), (1,)), ((), ())), preferred_element_type=F32)


def _rms(x, g):
    ms = jnp.mean(x * x, axis=-1, keepdims=True)
    return x * lax.rsqrt(ms + EPS) * g


def _log_sigmoid(x):
    return jnp.minimum(x, 0.0) - jnp.log1p(jnp.exp(-jnp.abs(x)))


def _split2(a):
    hi = a.astype(BF16)
    lo = (a - hi.astype(F32)).astype(BF16)
    return hi, lo


def _dot_split3(a, m01):
    a1 = a.astype(BF16)
    r1 = a - a1.astype(F32)
    a2 = r1.astype(BF16)
    a3 = (r1 - a2.astype(F32)).astype(BF16)
    return _dot(a1, m01) + _dot(a2, m01) + _dot(a3, m01)


def _tri(n, lower_eq):
    r = lax.broadcasted_iota(jnp.int32, (n, n), 0)
    c = lax.broadcasted_iota(jnp.int32, (n, n), 1)
    m = (r >= c) if lower_eq else (r <= c)
    return jnp.where(m, 1.0, 0.0).astype(BF16)


def _lane_tile(a, reps):
    return a if reps == 1 else jnp.concatenate([a] * reps, axis=1)


def _ffn_kernel(x_ref, g_ref, wg_ref, wu_ref, wo_ref, o_ref, h_ref):
    @pl.when(pl.program_id(1) == 0)
    def _():
        x = x_ref[...]
        h_ref[...] = _rms(x, g_ref[...]).astype(BF16)
        o_ref[...] = x

    h = h_ref[...]
    gate = _dot(h, wg_ref[...])
    up = _dot(h, wu_ref[...])
    a = (gate * jax.nn.sigmoid(gate)) * (0.5 * up)
    o_ref[...] += _dot(a.astype(BF16), wo_ref[...])


def _ffn(x, g, w_in, w_out):
    t, d = x.shape
    f = w_out.shape[0]
    tm = _pick_tile(t, 768, LANES)
    tf = _pick_tile(f, 512, LANES)
    nf = f // tf
    return pl.pallas_call(
        _ffn_kernel,
        grid=(t // tm, nf),
        in_specs=[
            pl.BlockSpec((tm, d), lambda i, j: (i, 0)),
            pl.BlockSpec((1, d), lambda i, j: (0, 0)),
            pl.BlockSpec((d, tf), lambda i, j: (0, j)),
            pl.BlockSpec((d, tf), lambda i, j: (0, nf + j)),
            pl.BlockSpec((tf, d), lambda i, j: (j, 0)),
        ],
        out_specs=pl.BlockSpec((tm, d), lambda i, j: (i, 0)),
        out_shape=jax.ShapeDtypeStruct((t, d), F32),
        scratch_shapes=[pltpu.VMEM((tm, d), BF16)],
        compiler_params=_params("parallel", "arbitrary"),
        name="ffn",
    )(x, g.reshape(1, d), w_in, w_in, w_out)


def _proj_even_kernel(x_ref, g_ref, w_ref, wf_ref, bf_ref, p_ref, lf_ref, lft_ref, h_ref):
    @pl.when(pl.program_id(1) == 0)
    def _():
        hb = _rms(x_ref[...], g_ref[...]).astype(BF16)
        h_ref[...] = hb
        lf = _log_sigmoid(_dot(hb, wf_ref[...]) + bf_ref[...])
        lf_ref[...] = lf
        lft_ref[...] = lf.T[:SUBLANES, :]

    p_ref[...] = _dot(h_ref[...], w_ref[...])


def _proj_even(x, g, w_main, w_f, b_f):
    t, d = x.shape
    n = w_main.shape[1]
    tm = _pick_tile(t, 768, LANES)
    tn = _pick_tile(n, 512, LANES)
    return pl.pallas_call(
        _proj_even_kernel,
        grid=(t // tm, n // tn),
        in_specs=[
            pl.BlockSpec((tm, d), lambda i, j: (i, 0)),
            pl.BlockSpec((1, d), lambda i, j: (0, 0)),
            pl.BlockSpec((d, tn), lambda i, j: (0, j)),
            pl.BlockSpec((d, LANES), lambda i, j: (0, 0)),
            pl.BlockSpec((1, LANES), lambda i, j: (0, 0)),
        ],
        out_specs=[
            pl.BlockSpec((tm, tn), lambda i, j: (i, j)),
            pl.BlockSpec((tm, LANES), lambda i, j: (i, 0)),
            pl.BlockSpec((SUBLANES, tm), lambda i, j: (0, i)),
        ],
        out_shape=[
            jax.ShapeDtypeStruct((t, n), F32),
            jax.ShapeDtypeStruct((t, LANES), F32),
            jax.ShapeDtypeStruct((SUBLANES, t), F32),
        ],
        scratch_shapes=[pltpu.VMEM((tm, d), BF16)],
        compiler_params=_params("parallel", "arbitrary"),
        name="proj_even",
    )(x, g.reshape(1, d), w_main, w_f, b_f)


def _fcum_kernel(lft_ref, o_ref, carry_ref):
    @pl.when(pl.program_id(1) == 0)
    def _():
        carry_ref[...] = jnp.zeros_like(carry_ref)

    blk = lft_ref.shape[1]
    cum = _dot_split3(lft_ref[...], _tri(blk, lower_eq=False)) + carry_ref[:, 0:1]
    o_ref[...] = cum
    carry_ref[...] = jnp.broadcast_to(cum[:, blk - 1:blk], carry_ref.shape)


def _fcum(lft, b, s):
    blk = _pick_tile(s, 512, LANES)
    nb = s // blk
    return pl.pallas_call(
        _fcum_kernel,
        grid=(b, nb),
        in_specs=[pl.BlockSpec((SUBLANES, blk), lambda i, j: (0, i * nb + j))],
        out_specs=pl.BlockSpec((SUBLANES, blk), lambda i, j: (0, i * nb + j)),
        out_shape=jax.ShapeDtypeStruct((SUBLANES, b * s), F32),
        scratch_shapes=[pltpu.VMEM((SUBLANES, LANES), F32)],
        compiler_params=_params("parallel", "arbitrary"),
        name="fcum",
    )(lft)


def _sb_block(q2, k_ref, v_ref, start, sub, mask, m_inc, c, acc_ref, scale):
    ks = k_ref[pl.ds(start, sub), :].astype(BF16)
    vs = v_ref[pl.ds(start, sub), :].astype(BF16)
    z = _dot_nt(q2, ks) * scale
    lk = _log_sigmoid(-z)
    if mask is not None:
        lk = jnp.where(mask, lk, 0.0)
    hi, lo = _split2(lk)
    inc = _dot(hi, m_inc) + _dot(lo, m_inc)
    w = jnp.exp(z + _lane_tile(c, sub // LANES) + inc)
    if mask is not None:
        w = jnp.where(mask, w, 0.0)
    acc_ref[...] += _dot(w.astype(BF16), vs)
    return c + jnp.broadcast_to(inc[:, 0:1], c.shape)


def _fox_block(q2, k_ref, v_ref, fk, start, sub, mask, m_ref, l_ref, acc_ref, scale):
    ks = k_ref[pl.ds(start, sub), :].astype(BF16)
    vs = v_ref[pl.ds(start, sub), :].astype(BF16)
    z = _dot_nt(q2, ks) * scale - fk
    if mask is not None:
        z = jnp.where(mask, z, NEG_INF)
    m_prev = m_ref[...]
    m_new = jnp.maximum(m_prev, jnp.max(z, axis=1, keepdims=True))
    alpha = jnp.exp(m_prev - m_new)
    p = jnp.exp(z - _lane_tile(m_new, sub // LANES))
    l_ref[...] = alpha * l_ref[...] + jnp.sum(p, axis=1, keepdims=True)
    acc_ref[...] = alpha * acc_ref[...] + _dot(p.astype(BF16), vs)
    m_ref[...] = m_new


def _prompt_attn_kernel(*refs, mode, g, tq, sub, scale):
    if mode == "sb":
        q_ref, k_ref, v_ref, o_ref, acc_ref, c_ref = refs
    else:
        q_ref, k_ref, v_ref, *fk_refs, o_ref, acc_ref, m_ref, l_ref = refs
    i = pl.program_id(2)
    nd = tq // sub
    rows = g * tq
    q = q_ref[...]
    q2 = jnp.concatenate([q[:, h * HEAD_DIM:(h + 1) * HEAD_DIM] for h in range(g)], axis=0).astype(BF16)
    qi = lax.broadcasted_iota(jnp.int32, (rows, sub), 0) & (tq - 1)
    kj = lax.broadcasted_iota(jnp.int32, (rows, sub), 1)
    acc_ref[...] = jnp.zeros_like(acc_ref)

    if mode == "sb":
        m_inc = _tri(sub, lower_eq=True)
        c = jnp.zeros((rows, LANES), F32)
        for d in reversed(range(nd)):
            start = pl.multiple_of(i * tq + d * sub, sub)
            c = _sb_block(q2, k_ref, v_ref, start, sub, kj + d * sub < qi, m_inc, c, acc_ref, scale)

        def body(jj, c):
            start = pl.multiple_of((i * nd - 1 - jj) * sub, sub)
            return _sb_block(q2, k_ref, v_ref, start, sub, None, m_inc, c, acc_ref, scale)

        lax.fori_loop(0, i * nd, body, c)
        out = acc_ref[...]
    else:
        m_ref[...] = jnp.full_like(m_ref, NEG_INF)
        l_ref[...] = jnp.zeros_like(l_ref)

        def fk_rows(start):
            return jnp.concatenate(
                [jnp.broadcast_to(r[:, pl.ds(start, sub)], (tq, sub)) for r in fk_refs], axis=0)

        for d in range(nd):
            start = pl.multiple_of(i * tq + d * sub, sub)
            _fox_block(q2, k_ref, v_ref, fk_rows(start), start, sub, kj + d * sub <= qi,
                       m_ref, l_ref, acc_ref, scale)

        def body(jj, carry):
            start = pl.multiple_of(jj * sub, sub)
            _fox_block(q2, k_ref, v_ref, fk_rows(start), start, sub, None, m_ref, l_ref, acc_ref, scale)
            return carry

        lax.fori_loop(0, i * nd, body, 0)
        out = acc_ref[...] / l_ref[...]

    o_ref[...] = jnp.concatenate([out[h * tq:(h + 1) * tq] for h in range(g)], axis=1).astype(o_ref.dtype)


def _prompt_attn(p, frow, mode, b, s, kv, g, qcol, kcol, vcol):
    assert s & (s - 1) == 0
    tq = min(512, s)
    sub = min(256, tq)
    nq = s // tq
    gw = g * HEAD_DIM
    in_specs = [
        pl.BlockSpec((tq, gw), lambda bi, k, i: (bi * nq + i, qcol // gw + k)),
        pl.BlockSpec((s, HEAD_DIM), lambda bi, k, i: (bi, kcol // HEAD_DIM + k)),
        pl.BlockSpec((s, HEAD_DIM), lambda bi, k, i: (bi, vcol // HEAD_DIM + k)),
    ]
    args = [p, p, p]
    scratch = [pltpu.VMEM((g * tq, HEAD_DIM), F32)]
    if mode == "sb":
        scratch += [pltpu.VMEM((g * tq, LANES), F32)]
    else:
        for h in range(g):
            in_specs.append(pl.BlockSpec((None, 1, s), lambda bi, k, i, h=h: (k * g + h, 0, bi)))
            args.append(frow)
        scratch += [pltpu.VMEM((g * tq, LANES), F32), pltpu.VMEM((g * tq, LANES), F32)]
    return pl.pallas_call(
        functools.partial(_prompt_attn_kernel, mode=mode, g=g, tq=tq, sub=sub, scale=HEAD_DIM ** -0.5),
        grid=(b, kv, nq),
        in_specs=in_specs,
        out_specs=pl.BlockSpec((tq, gw), lambda bi, k, i: (bi * nq + i, k)),
        out_shape=jax.ShapeDtypeStruct((b * s, kv * gw), BF16),
        scratch_shapes=scratch,
        compiler_params=_params("parallel", "parallel", "parallel"),
        name="prompt_" + mode,
    )(*args)


def _even_sample_kernel(pt_ref, qa_ref, kan_ref, van_ref, qb_ref, kbn_ref, vbn_ref, lfn_ref, *rest,
                        pp, kv, g, tt, page, scale):
    ka_pg, va_pg, kb_pg, vb_pg, lf_pg = (rest[n * pp:(n + 1) * pp] for n in range(5))
    oa_ref, ob_ref, acca_ref, c_ref, accb_ref, m_ref, l_ref, rc_ref = rest[5 * pp:]
    del pt_ref
    s = pl.program_id(1)
    nh = kv * g
    rows = nh * tt
    gr = g * tt
    m_inc = _tri(page, lower_eq=True)

    def stack_heads(q):
        return jnp.concatenate([q[:, h * HEAD_DIM:(h + 1) * HEAD_DIM] for h in range(nh)], axis=0).astype(BF16)

    def rep_rows(a):
        return jnp.concatenate([jnp.broadcast_to(a[h:h + 1, :], (tt, a.shape[1])) for h in range(nh)], axis=0)

    qa = stack_heads(qa_ref[...])
    qb = stack_heads(qb_ref[...])

    def scores(q, kblocks):
        return jnp.concatenate(
            [_dot_nt(q[k * gr:(k + 1) * gr], kblocks[k]) for k in range(kv)], axis=0) * scale

    def weighted(w, vblocks):
        wb = w.astype(BF16)
        return jnp.concatenate(
            [_dot(wb[k * gr:(k + 1) * gr], vblocks[k]) for k in range(kv)], axis=0)

    @pl.when(s == 0)
    def _():
        zpad = jnp.zeros((page - tt, HEAD_DIM), F32)

        def new_blocks(ref):
            x = ref[...]
            return [jnp.concatenate([x[:, k * HEAD_DIM:(k + 1) * HEAD_DIM], zpad], axis=0).astype(BF16)
                    for k in range(kv)]

        t_of = lax.broadcasted_iota(jnp.int32, (rows, page), 0) & (tt - 1)
        col = lax.broadcasted_iota(jnp.int32, (rows, page), 1)
        mask = col < t_of
        z = scores(qa, new_blocks(kan_ref))
        lk = jnp.where(mask, _log_sigmoid(-z), 0.0)
        hi, lo = _split2(lk)
        inc = _dot(hi, m_inc) + _dot(lo, m_inc)
        w = jnp.where(mask, jnp.exp(z + inc), 0.0)
        acca_ref[...] = weighted(w, new_blocks(van_ref))
        c_ref[...] = jnp.broadcast_to(inc[:, 0:1], c_ref.shape)
        fnew = _dot_split3(lfn_ref[...], _tri(page, lower_eq=False))
        zb = scores(qb, new_blocks(kbn_ref)) - rep_rows(fnew)
        zb = jnp.where(col <= t_of, zb, NEG_INF)
        m = jnp.max(zb, axis=1, keepdims=True)
        p = jnp.exp(zb - m)
        m_ref[...] = jnp.broadcast_to(m, m_ref.shape)
        l_ref[...] = jnp.broadcast_to(jnp.sum(p, axis=1, keepdims=True), l_ref.shape)
        accb_ref[...] = weighted(p, new_blocks(vbn_ref))
        rc_ref[...] = jnp.zeros_like(rc_ref)

    def page_blocks(ref):
        return [ref[pl.ds(k, page, stride=kv), :].astype(BF16) for k in range(kv)]

    za = [scores(qa, page_blocks(ka_pg[u])) for u in range(pp)]
    lka = [_log_sigmoid(-z) for z in za]
    inca = []
    for lk in lka:
        hi, lo = _split2(lk)
        inca.append(_dot(hi, m_inc) + _dot(lo, m_inc))
    c = c_ref[...]
    acc = acca_ref[...]
    for u in reversed(range(pp)):
        w = jnp.exp(za[u] + c + inca[u])
        acc = acc + weighted(w, page_blocks(va_pg[u]))
        c = c + jnp.broadcast_to(inca[u][:, 0:1], c.shape)
    c_ref[...] = c
    acca_ref[...] = acc

    rc = rc_ref[...]
    zb = [None] * pp
    for u in reversed(range(pp)):
        lf = lf_pg[u][...]
        incf = _dot_split3(lf, m_inc)
        bias = rc + incf - lf
        rc = rc + jnp.broadcast_to(incf[:, 0:1], rc.shape)
        zb[u] = scores(qb, page_blocks(kb_pg[u])) + rep_rows(bias)
    rc_ref[...] = rc
    zb = jnp.concatenate(zb, axis=1)
    m_prev = m_ref[...]
    m_new = jnp.maximum(m_prev, jnp.max(zb, axis=1, keepdims=True))
    alpha = jnp.exp(m_prev - m_new)
    p = jnp.exp(zb - _lane_tile(m_new, pp))
    l_ref[...] = alpha * l_ref[...] + jnp.sum(p, axis=1, keepdims=True)
    accb = alpha * accb_ref[...]
    for u in range(pp):
        accb = accb + weighted(p[:, u * page:(u + 1) * page], page_blocks(vb_pg[u]))
    accb_ref[...] = accb
    m_ref[...] = m_new

    @pl.when(s == pl.num_programs(1) - 1)
    def _():
        def unstack(a):
            return jnp.concatenate([a[h * tt:(h + 1) * tt] for h in range(nh)], axis=1)

        oa_ref[...] = unstack(acca_ref[...])
        ob_ref[...] = unstack(accb_ref[...] / l_ref[...])


def _even_sample(p, lfn, caches, lf_cache, page_table, layer, row0, kv, g, tt, cols):
    db, npg = page_table.shape
    page = caches[0].shape[2] // kv
    nh = kv * g
    pp = 4 if npg % 4 == 0 else 1
    ns = npg // pp
    qw = nh * HEAD_DIM
    kw = kv * HEAD_DIM
    rb = row0 // tt
    qa_c, ka_c, va_c, qb_c, kb_c, vb_c = cols

    def tok(width, col):
        return pl.BlockSpec((tt, width), lambda b, s, pt: (rb + b, col // width))

    in_specs = [tok(qw, qa_c), tok(kw, ka_c), tok(kw, va_c), tok(qw, qb_c), tok(kw, kb_c), tok(kw, vb_c),
                pl.BlockSpec((None, nh, LANES), lambda b, s, pt: (b, 0, 0))]
    args = [p] * 6 + [lfn]

    def page_index(u):
        return lambda b, s, pt: (layer, pt[b * npg + npg - (s + 1) * pp + u], 0, 0)

    for cache in caches:
        for u in range(pp):
            in_specs.append(pl.BlockSpec((None, None, page * kv, HEAD_DIM), page_index(u)))
            args.append(cache)
    for u in range(pp):
        in_specs.append(pl.BlockSpec((None, None, nh, page), page_index(u)))
        args.append(lf_cache)

    rows = nh * tt
    out = pl.pallas_call(
        functools.partial(_even_sample_kernel, pp=pp, kv=kv, g=g, tt=tt, page=page, scale=HEAD_DIM ** -0.5),
        grid_spec=pltpu.PrefetchScalarGridSpec(
            num_scalar_prefetch=1,
            grid=(db, ns),
            in_specs=in_specs,
            out_specs=[pl.BlockSpec((None, tt, qw), lambda b, s, pt: (b, 0, 0)),
                       pl.BlockSpec((None, tt, qw), lambda b, s, pt: (b, 0, 0))],
            scratch_shapes=[pltpu.VMEM((rows, HEAD_DIM), F32), pltpu.VMEM((rows, LANES), F32),
                            pltpu.VMEM((rows, HEAD_DIM), F32), pltpu.VMEM((rows, LANES), F32),
                            pltpu.VMEM((rows, LANES), F32), pltpu.VMEM((nh, LANES), F32)],
        ),
        out_shape=[jax.ShapeDtypeStruct((db, tt, qw), F32), jax.ShapeDtypeStruct((db, tt, qw), F32)],
        compiler_params=_params("parallel", "arbitrary"),
        name="even_sample",
    )(page_table.reshape(-1), *args)
    return out


def _out_proj_kernel(a1_ref, a2_ref, w1_ref, w2_ref, x_ref, o_ref):
    o_ref[...] = x_ref[...] + _dot(a1_ref[...], w1_ref[...]) + _dot(a2_ref[...], w2_ref[...])


def _out_proj(a1, a2, col2, w, x):
    t, d = x.shape
    kh = w.shape[0] // 2
    tm = _pick_tile(t, 768, LANES)
    tn = _pick_tile(d, 512, LANES)
    return pl.pallas_call(
        _out_proj_kernel,
        grid=(t // tm, d // tn),
        in_specs=[
            pl.BlockSpec((tm, kh), lambda i, j: (i, 0)),
            pl.BlockSpec((tm, kh), lambda i, j: (i, col2)),
            pl.BlockSpec((kh, tn), lambda i, j: (0, j)),
            pl.BlockSpec((kh, tn), lambda i, j: (1, j)),
            pl.BlockSpec((tm, tn), lambda i, j: (i, j)),
        ],
        out_specs=pl.BlockSpec((tm, tn), lambda i, j: (i, j)),
        out_shape=jax.ShapeDtypeStruct((t, d), F32),
        compiler_params=_params("parallel", "parallel"),
        name="out_proj",
    )(a1, a2, w, w, x)


def _rope_table_kernel(pos_ref, inv_ref, cos_ref, sin_ref):
    ang = pos_ref[...] * inv_ref[...]
    lane = lax.broadcasted_iota(jnp.int32, ang.shape, 1)
    first = (lane & (HD_C - 1)) < HD_C // 2
    cos_ref[...] = jnp.cos(ang)
    sin_ref[...] = jnp.where(first, -1.0, 1.0) * jnp.sin(ang)


def _rope_table(pos, inv):
    t = pos.shape[0]
    tm = _pick_tile(t, 768, LANES)
    return pl.pallas_call(
        _rope_table_kernel,
        grid=(t // tm,),
        in_specs=[pl.BlockSpec((tm, LANES), lambda i: (i, 0)), pl.BlockSpec((1, LANES), lambda i: (0, 0))],
        out_specs=[pl.BlockSpec((tm, LANES), lambda i: (i, 0))] * 2,
        out_shape=[jax.ShapeDtypeStruct((t, LANES), F32)] * 2,
        compiler_params=_params("parallel"),
        name="rope_table",
    )(pos, inv)


def _proj_odd_kernel(x_ref, g_ref, w_ref, cos_ref, sin_ref, p_ref, h_ref, *, rope_cols):
    j = pl.program_id(1)

    @pl.when(j == 0)
    def _():
        h_ref[...] = _rms(x_ref[...], g_ref[...]).astype(BF16)

    y = _dot(h_ref[...], w_ref[...])
    tn = y.shape[1]
    half = HD_C // 2
    lane = lax.broadcasted_iota(jnp.int32, y.shape, 1)
    first = (lane & (HD_C - 1)) < half
    partner = jnp.where(first, pltpu.roll(y, tn - half, 1), pltpu.roll(y, half, 1))
    reps = tn // LANES
    roped = y * _lane_tile(cos_ref[...], reps) + partner * _lane_tile(sin_ref[...], reps)
    p_ref[...] = jnp.where(lane + j * tn < rope_cols, roped, y)


def _proj_odd(x, g, w, cos_t, sin_t, rope_cols):
    t, d = x.shape
    n = w.shape[1]
    tm = _pick_tile(t, 768, LANES)
    tn = _pick_tile(n, 512, LANES)
    return pl.pallas_call(
        functools.partial(_proj_odd_kernel, rope_cols=rope_cols),
        grid=(t // tm, n // tn),
        in_specs=[
            pl.BlockSpec((tm, d), lambda i, j: (i, 0)),
            pl.BlockSpec((1, d), lambda i, j: (0, 0)),
            pl.BlockSpec((d, tn), lambda i, j: (0, j)),
            pl.BlockSpec((tm, LANES), lambda i, j: (i, 0)),
            pl.BlockSpec((tm, LANES), lambda i, j: (i, 0)),
        ],
        out_specs=pl.BlockSpec((tm, tn), lambda i, j: (i, j)),
        out_shape=jax.ShapeDtypeStruct((t, n), F32),
        scratch_shapes=[pltpu.VMEM((tm, d), BF16)],
        compiler_params=_params("parallel", "arbitrary"),
        name="proj_odd",
    )(x, g.reshape(1, d), w, cos_t, sin_t)


def _swa_group(q, kband, vband, mask, sink_col, g, scale):
    tq = q.shape[0]
    q8 = jnp.concatenate([q[:, h * HD_C:(h + 1) * HD_C] for h in range(g)], axis=0).astype(BF16)
    z = _dot_nt(q8, kband.astype(BF16)) * scale
    z = jnp.where(mask, z, NEG_INF)
    m = jnp.maximum(jnp.max(z, axis=1, keepdims=True), sink_col)
    p = jnp.exp(z - m)
    denom = jnp.sum(p, axis=1, keepdims=True) + jnp.exp(sink_col - m)
    o = _dot(p.astype(BF16), vband.astype(BF16)) / denom
    return jnp.concatenate([o[h * tq:(h + 1) * tq] for h in range(g)], axis=1)


def _sink_col(sinks_ref, k, g, tq):
    return jnp.concatenate([jnp.full((tq, 1), sinks_ref[k * g + h], F32) for h in range(g)], axis=0)


def _swa_prompt_kernel(sinks_ref, q_ref, kc_ref, kp_ref, vc_ref, vp_ref, o_ref, *, kv, g, scale):
    i = pl.program_id(1)
    tq = q_ref.shape[0]
    gw = g * HD_C
    qi = lax.broadcasted_iota(jnp.int32, (g * tq, 2 * tq), 0) & (tq - 1)
    kj = lax.broadcasted_iota(jnp.int32, (g * tq, 2 * tq), 1)
    mask = (kj > qi) & (kj <= qi + tq) & ((kj >= tq) | (i > 0))
    for k in range(kv):
        cs = slice(k * HD_C, (k + 1) * HD_C)
        kband = jnp.concatenate([kp_ref[:, cs], kc_ref[:, cs]], axis=0)
        vband = jnp.concatenate([vp_ref[:, cs], vc_ref[:, cs]], axis=0)
        o = _swa_group(q_ref[:, k * gw:(k + 1) * gw], kband, vband, mask, _sink_col(sinks_ref, k, g, tq), g, scale)
        o_ref[:, k * gw:(k + 1) * gw] = o.astype(o_ref.dtype)


def _swa_prompt(p, sinks, b, s, kv, g):
    nq = s // WINDOW
    qw = kv * g * HD_C
    kw = kv * HD_C
    kb = qw // kw

    def cur(col):
        return pl.BlockSpec((WINDOW, kw), lambda bi, i, sk: (bi * nq + i, col))

    def prev(col):
        return pl.BlockSpec((WINDOW, kw), lambda bi, i, sk: (jnp.maximum(bi * nq + i - 1, 0), col))

    return pl.pallas_call(
        functools.partial(_swa_prompt_kernel, kv=kv, g=g, scale=HD_C ** -0.5),
        grid_spec=pltpu.PrefetchScalarGridSpec(
            num_scalar_prefetch=1,
            grid=(b, nq),
            in_specs=[pl.BlockSpec((WINDOW, qw), lambda bi, i, sk: (bi * nq + i, 0)),
                      cur(kb), prev(kb), cur(kb + 1), prev(kb + 1)],
            out_specs=pl.BlockSpec((WINDOW, qw), lambda bi, i, sk: (bi * nq + i, 0)),
        ),
        out_shape=jax.ShapeDtypeStruct((b * s, qw), BF16),
        compiler_params=_params("parallel", "parallel"),
        name="swa_prompt",
    )(sinks, p, p, p, p, p)


def _swa_sample_kernel(sinks_ref, q_ref, kn_ref, vn_ref, kbuf_ref, vbuf_ref, o_ref, *, kv, g, tt, scale):
    gw = g * HD_C
    nk = 2 * WINDOW
    t_of = lax.broadcasted_iota(jnp.int32, (g * tt, nk), 0) & (tt - 1)
    col = lax.broadcasted_iota(jnp.int32, (g * tt, nk), 1)
    mask = ((col < WINDOW) & (col > t_of)) | ((col >= WINDOW) & (col - WINDOW <= t_of))
    zpad = jnp.zeros((WINDOW - tt, HD_C), F32)
    for k in range(kv):
        cs = slice(k * HD_C, (k + 1) * HD_C)
        kband = jnp.concatenate([kbuf_ref[:, cs], kn_ref[:, cs], zpad], axis=0)
        vband = jnp.concatenate([vbuf_ref[:, cs], vn_ref[:, cs], zpad], axis=0)
        o = _swa_group(q_ref[:, k * gw:(k + 1) * gw], kband, vband, mask, _sink_col(sinks_ref, k, g, tt), g, scale)
        o_ref[:, k * gw:(k + 1) * gw] = o


def _swa_sample(p, sinks, kbuf, vbuf, layer, row0, kv, g, tt):
    db = kbuf.shape[1]
    qw = kv * g * HD_C
    kw = kv * HD_C
    kb = qw // kw
    rb = row0 // tt
    buf_spec = pl.BlockSpec((None, None, WINDOW, kw), lambda b, sk: (layer, b, 0, 0))
    return pl.pallas_call(
        functools.partial(_swa_sample_kernel, kv=kv, g=g, tt=tt, scale=HD_C ** -0.5),
        grid_spec=pltpu.PrefetchScalarGridSpec(
            num_scalar_prefetch=1,
            grid=(db,),
            in_specs=[pl.BlockSpec((tt, qw), lambda b, sk: (rb + b, 0)),
                      pl.BlockSpec((tt, kw), lambda b, sk: (rb + b, kb)),
                      pl.BlockSpec((tt, kw), lambda b, sk: (rb + b, kb + 1)),
                      buf_spec, buf_spec],
            out_specs=pl.BlockSpec((None, tt, qw), lambda b, sk: (b, 0, 0)),
        ),
        out_shape=jax.ShapeDtypeStruct((db, tt, qw), F32),
        compiler_params=_params("parallel"),
        name="swa_sample",
    )(sinks, p, p, p, kbuf, vbuf)


def _final_norm_kernel(x_ref, g_ref, o_ref):
    o_ref[...] = _rms(x_ref[...], g_ref[...])


def _final_norm(x, g):
    t, d = x.shape
    tm = _pick_tile(t, 768, LANES)
    return pl.pallas_call(
        _final_norm_kernel,
        grid=(t // tm,),
        in_specs=[pl.BlockSpec((tm, d), lambda i: (i, 0)), pl.BlockSpec((1, d), lambda i: (0, 0))],
        out_specs=pl.BlockSpec((tm, d), lambda i: (i, 0)),
        out_shape=jax.ShapeDtypeStruct((t, d), F32),
        compiler_params=_params("parallel"),
        name="final_norm",
    )(x, g.reshape(1, d))


def kernel(x_prompt, x_sample, cache_sb_k, cache_sb_v, cache_fox_k, cache_fox_v, cache_fox_logf, cache_swa_k, cache_swa_v, page_table, norm_ffn1, norm_mix, norm_ffn2, norm_final, w_ffn_in, w_ffn_out, w_in_even, b_forget, w_out_even, w_in_odd, sinks, w_out_odd):
    b, s, d = x_prompt.shape
    db, tt, _ = x_sample.shape
    depth = norm_mix.shape[0]
    n_even, n_phys, page, kv_a, _ = cache_sb_k.shape
    n_odd = cache_swa_k.shape[0]
    kv_c = cache_swa_k.shape[3]
    h_b = cache_fox_logf.shape[3]
    kv_b = cache_fox_k.shape[3]
    h_a = (w_in_even.shape[2] - h_b - 2 * (kv_a + kv_b) * HEAD_DIM) // HEAD_DIM - h_b
    g_a, g_b = h_a // kv_a, h_b // kv_b
    h_c = w_out_odd.shape[1] // HD_C
    g_c = h_c // kv_c
    assert kv_a == kv_b and g_a == g_b and h_b == SUBLANES and tt == SUBLANES
    bs = b * s
    past_len = page_table.shape[1] * page

    qa_c = 0
    ka_c = qa_c + h_a * HEAD_DIM
    va_c = ka_c + kv_a * HEAD_DIM
    qb_c = va_c + kv_a * HEAD_DIM
    kb_c = qb_c + h_b * HEAD_DIM
    vb_c = kb_c + kv_b * HEAD_DIM
    n_main = vb_c + kv_b * HEAD_DIM
    kw = kv_a * HEAD_DIM

    x = jnp.concatenate([x_prompt.reshape(bs, d), x_sample.reshape(db * tt, d)], axis=0)
    w_ffn_in = w_ffn_in.astype(BF16)
    w_ffn_out = w_ffn_out.astype(BF16)

    paged = [c.reshape(n_even, n_phys, page * kv_a, HEAD_DIM) for c in (cache_sb_k, cache_sb_v, cache_fox_k, cache_fox_v)]
    lf_cache = jnp.swapaxes(cache_fox_logf, 2, 3)
    swa_kbuf = cache_swa_k.reshape(n_odd, db, WINDOW, kv_c * HD_C)
    swa_vbuf = cache_swa_v.reshape(n_odd, db, WINDOW, kv_c * HD_C)

    half = HD_C // 2
    inv_freq = ROPE_THETA ** (-jnp.arange(half, dtype=F32) / half)
    inv = jnp.tile(inv_freq, LANES // half).reshape(1, LANES)
    pos = jnp.concatenate([jnp.tile(jnp.arange(s, dtype=jnp.int32), b),
                           jnp.tile(past_len + jnp.arange(tt, dtype=jnp.int32), db)]).astype(F32)
    cos_t, sin_t = _rope_table(jnp.broadcast_to(pos[:, None], (bs + db * tt, LANES)), inv)

    even_p = [[] for _ in range(5)]
    even_s = [[] for _ in range(5)]
    odd_p = [[], []]
    odd_s = [[], []]

    for l in range(depth):
        x = _ffn(x, norm_ffn1[l], w_ffn_in[l, 0], w_ffn_out[l, 0])
        i = l // 2
        if l % 2 == 0:
            w_in = w_in_even[i]
            w_main = w_in[:, :n_main].astype(BF16)
            w_f = jnp.pad(w_in[:, n_main:], ((0, 0), (0, LANES - h_b))).astype(BF16)
            b_f = jnp.pad(b_forget[i], (0, LANES - h_b)).reshape(1, LANES)
            p, lf, lft = _proj_even(x, norm_mix[l], w_main, w_f, b_f)
            frow = _fcum(lft, b, s).reshape(h_b, 1, bs)
            oa_p = _prompt_attn(p, None, "sb", b, s, kv_a, g_a, qa_c, ka_c, va_c)
            ob_p = _prompt_attn(p, frow, "fox", b, s, kv_b, g_b, qb_c, kb_c, vb_c)
            lfn = jnp.pad(lft[:, bs:].reshape(h_b, db, tt).transpose(1, 0, 2), ((0, 0), (0, 0), (0, LANES - tt)))
            oa_s, ob_s = _even_sample(p, lfn, paged, lf_cache, page_table, i, bs, kv_a, g_a, tt,
                                      (qa_c, ka_c, va_c, qb_c, kb_c, vb_c))
            oa = jnp.concatenate([oa_p, oa_s.reshape(db * tt, -1).astype(BF16)], axis=0)
            ob = jnp.concatenate([ob_p, ob_s.reshape(db * tt, -1).astype(BF16)], axis=0)
            x = _out_proj(oa, ob, 0, w_out_even[i].astype(BF16), x)
            news = [p[:, ka_c:ka_c + kw], p[:, va_c:va_c + kw], p[:, kb_c:kb_c + kw], p[:, vb_c:vb_c + kw]]
            for n, arr in enumerate(news):
                even_p[n].append(arr[:bs].reshape(b, s, kv_a, HEAD_DIM))
                even_s[n].append(arr[bs:].reshape(db, tt, kv_a, HEAD_DIM))
            even_p[4].append(lf[:bs, :h_b].reshape(b, s, h_b))
            even_s[4].append(lf[bs:, :h_b].reshape(db, tt, h_b))
        else:
            qw = h_c * HD_C
            kwc = kv_c * HD_C
            p = _proj_odd(x, norm_mix[l], w_in_odd[i].astype(BF16), cos_t, sin_t, qw + kwc)
            o_p = _swa_prompt(p, sinks[i], b, s, kv_c, g_c)
            o_s = _swa_sample(p, sinks[i], swa_kbuf, swa_vbuf, i, bs, kv_c, g_c, tt)
            o = jnp.concatenate([o_p, o_s.reshape(db * tt, -1).astype(BF16)], axis=0)
            x = _out_proj(o, o, 1, w_out_odd[i].astype(BF16), x)
            k_new = p[:, qw:qw + kwc]
            v_new = p[:, qw + kwc:]
            for n, (arr, buf) in enumerate(((k_new, cache_swa_k[i]), (v_new, cache_swa_v[i]))):
                odd_p[n].append(arr[:bs].reshape(b, s, kv_c, HD_C)[:, s - WINDOW:])
                odd_s[n].append(jnp.concatenate([buf[:, tt:], arr[bs:].reshape(db, tt, kv_c, HD_C)], axis=1))
        x = _ffn(x, norm_ffn2[l], w_ffn_in[l, 1], w_ffn_out[l, 1])

    y = _final_norm(x, norm_final)
    outs = [y[:bs].reshape(b, s, d), y[bs:].reshape(db, tt, d)]
    outs += [jnp.stack(a) for a in even_p]
    outs += [jnp.stack(a) for a in odd_p]
    outs += [jnp.stack(a) for a in even_s]
    outs += [jnp.stack(a) for a in odd_s]
    return tuple(outs)
```

```python
import functools
import math

import jax
import jax.numpy as jnp
from jax import lax
from jax.experimental import pallas as pl
from jax.experimental.pallas import tpu as pltpu

F32 = jnp.float32
BF16 = jnp.bfloat16

HEAD_DIM = 128
HD_C = 64
WINDOW = 128
ROPE_THETA = 10000.0
EPS = 1e-6
NEG_INF = -1e30
LOG2E = math.log2(math.e)
LANES = 128
SUBLANES = 8
VMEM_LIMIT = 56 * 1024 * 1024
PAGES_PER_CHUNK = 8
CHUNK_SLOTS = 3


def _pick_tile(n, target, mult):
    best = None
    for t in range(mult, min(n, target) + 1, mult):
        if n % t == 0:
            best = t
    assert best is not None, (n, target, mult)
    return best


def _params(*sem):
    return pltpu.CompilerParams(dimension_semantics=sem, vmem_limit_bytes=VMEM_LIMIT)


def _dot(a, b):
    return jnp.dot(a, b, preferred_element_type=F32)


def _dot_nt(a, b):
    return lax.dot_general(a, b, (((1,), (1,)), ((), ())), preferred_element_type=F32)


def _rms(x, g):
    ms = jnp.mean(x * x, axis=-1, keepdims=True)
    return x * lax.rsqrt(ms + EPS) * g


def _log_sigmoid(x):
    return jnp.minimum(x, 0.0) - jnp.log1p(jnp.exp(-jnp.abs(x)))


def _log2_sigmoid(ny):
    neg_abs = lax.bitcast_convert_type(lax.bitcast_convert_type(ny, jnp.uint32) | jnp.uint32(0x80000000), F32)
    return jnp.minimum(ny, 0.0) - jnp.log2(1.0 + jnp.exp2(neg_abs))


def _split2(a):
    hi = a.astype(BF16)
    lo = (a - hi.astype(F32)).astype(BF16)
    return hi, lo


def _dot_split2(a, m01x2):
    return _dot(jnp.concatenate(_split2(a), axis=1), m01x2)


def _dot_split3(a, m01):
    a1 = a.astype(BF16)
    r1 = a - a1.astype(F32)
    a2 = r1.astype(BF16)
    a3 = (r1 - a2.astype(F32)).astype(BF16)
    return _dot(a1, m01) + _dot(a2, m01) + _dot(a3, m01)


def _tri(n, lower_eq, stack=1):
    r = lax.broadcasted_iota(jnp.int32, (stack * n, n), 0) & (n - 1)
    c = lax.broadcasted_iota(jnp.int32, (stack * n, n), 1)
    m = (r >= c) if lower_eq else (r <= c)
    return jnp.where(m, 1.0, 0.0).astype(BF16)


def _lane_tile(a, reps):
    return a if reps == 1 else jnp.concatenate([a] * reps, axis=1)


def _ffn_kernel(x_ref, g_ref, wg_ref, wu_ref, wo_ref, o_ref, h_ref):
    @pl.when(pl.program_id(1) == 0)
    def _():
        x = x_ref[...]
        h_ref[...] = _rms(x, g_ref[...]).astype(BF16)
        o_ref[...] = x

    h = h_ref[...]
    gate = _dot(h, wg_ref[...])
    up = _dot(h, wu_ref[...])
    a = (gate * jax.nn.sigmoid(gate)) * (0.5 * up)
    o_ref[...] += _dot(a.astype(BF16), wo_ref[...])


def _ffn(x, g, w_in, w_out):
    t, d = x.shape
    f = w_out.shape[0]
    tm = _pick_tile(t, 768, LANES)
    tf = _pick_tile(f, 512, LANES)
    nf = f // tf
    return pl.pallas_call(
        _ffn_kernel,
        grid=(t // tm, nf),
        in_specs=[
            pl.BlockSpec((tm, d), lambda i, j: (i, 0)),
            pl.BlockSpec((1, d), lambda i, j: (0, 0)),
            pl.BlockSpec((d, tf), lambda i, j: (0, j)),
            pl.BlockSpec((d, tf), lambda i, j: (0, nf + j)),
            pl.BlockSpec((tf, d), lambda i, j: (j, 0)),
        ],
        out_specs=pl.BlockSpec((tm, d), lambda i, j: (i, 0)),
        out_shape=jax.ShapeDtypeStruct((t, d), F32),
        scratch_shapes=[pltpu.VMEM((tm, d), BF16)],
        compiler_params=_params("parallel", "arbitrary"),
        name="ffn",
    )(x, g.reshape(1, d), w_in, w_in, w_out)


def _proj_even_kernel(*refs, nq, nkv):
    x_ref, g_ref, w_ref, wf_ref, bf_ref = refs[:5]
    outs = refs[5 + nkv:]
    q_ref, kv_refs, lf_ref, lft_ref, h_ref = outs[0], outs[1:1 + nkv], outs[1 + nkv], outs[2 + nkv], outs[3 + nkv]
    j = pl.program_id(1)

    @pl.when(j == 0)
    def _():
        hb = _rms(x_ref[...], g_ref[...]).astype(BF16)
        h_ref[...] = hb
        lf = _log_sigmoid(_dot(hb, wf_ref[...]) + bf_ref[...])
        lf_ref[...] = lf
        lft_ref[...] = lf.T[:SUBLANES, :]

    y = _dot(h_ref[...], w_ref[...])

    @pl.when(j < nq)
    def _():
        q_ref[...] = y

    for n, ref in enumerate(kv_refs):
        @pl.when(j == nq + n)
        def _(ref=ref):
            ref[...] = y


def _proj_even(x, row0, n, g, w_main, w_f, b_f, qw, kw, layer, n_layers, prev):
    d = x.shape[1]
    tm = _pick_tile(n, 1024, LANES)
    assert row0 % tm == 0 and qw % kw == 0
    r0 = row0 // tm
    nq = qw // kw
    nkv = (w_main.shape[1] - qw) // kw
    assert len(prev) == nkv
    in_specs = [
        pl.BlockSpec((tm, d), lambda i, j: (r0 + i, 0)),
        pl.BlockSpec((1, d), lambda i, j: (0, 0)),
        pl.BlockSpec((d, kw), lambda i, j: (0, j)),
        pl.BlockSpec((d, LANES), lambda i, j: (0, 0)),
        pl.BlockSpec((1, LANES), lambda i, j: (0, 0)),
    ]
    args = [x, g.reshape(1, d), w_main, w_f, b_f]
    aliases = {}
    for m, arr in enumerate(prev):
        in_specs.append(pl.BlockSpec(memory_space=pl.ANY))
        args.append(arr)
        aliases[5 + m] = 1 + m
    out_specs = [pl.BlockSpec((tm, kw), lambda i, j: (i, jnp.minimum(j, nq - 1)))]
    out_shape = [jax.ShapeDtypeStruct((n, qw), F32)]
    for _ in range(nkv):
        out_specs.append(pl.BlockSpec((None, tm, kw), lambda i, j: (layer, i, 0)))
        out_shape.append(jax.ShapeDtypeStruct((n_layers, n, kw), F32))
    out_specs += [pl.BlockSpec((tm, LANES), lambda i, j: (i, 0)), pl.BlockSpec((SUBLANES, tm), lambda i, j: (0, i))]
    out_shape += [jax.ShapeDtypeStruct((n, LANES), F32), jax.ShapeDtypeStruct((SUBLANES, n), F32)]
    outs = pl.pallas_call(
        functools.partial(_proj_even_kernel, nq=nq, nkv=nkv),
        grid=(n // tm, nq + nkv),
        in_specs=in_specs,
        out_specs=out_specs,
        out_shape=out_shape,
        scratch_shapes=[pltpu.VMEM((tm, d), BF16)],
        input_output_aliases=aliases,
        compiler_params=_params("parallel", "arbitrary"),
        name="proj_even",
    )(*args)
    return outs[0], list(outs[1:1 + nkv]), outs[1 + nkv], outs[2 + nkv]


def _fcum_kernel(lft_ref, o_ref, carry_ref):
    @pl.when(pl.program_id(1) == 0)
    def _():
        carry_ref[...] = jnp.zeros_like(carry_ref)

    blk = lft_ref.shape[1]
    cum = _dot_split3(lft_ref[...], _tri(blk, lower_eq=False)) + carry_ref[:, 0:1]
    o_ref[...] = cum
    carry_ref[...] = jnp.broadcast_to(cum[:, blk - 1:blk], carry_ref.shape)


def _fcum(lft, b, s):
    blk = _pick_tile(s, 512, LANES)
    nb = s // blk
    return pl.pallas_call(
        _fcum_kernel,
        grid=(b, nb),
        in_specs=[pl.BlockSpec((SUBLANES, blk), lambda i, j: (0, i * nb + j))],
        out_specs=pl.BlockSpec((SUBLANES, blk), lambda i, j: (0, i * nb + j)),
        out_shape=jax.ShapeDtypeStruct((SUBLANES, b * s), F32),
        scratch_shapes=[pltpu.VMEM((SUBLANES, LANES), F32)],
        compiler_params=_params("parallel", "arbitrary"),
        name="fcum",
    )(lft)


def _sb_block(q2, k_ref, v_ref, start, sub, mask, m_inc2, c, acc_ref, cs):
    ks = k_ref[pl.ds(start, sub), :].astype(BF16)
    vs = v_ref[pl.ds(start, sub), :].astype(BF16)
    ny = _dot_nt(q2, ks) * (-cs)
    lk = _log2_sigmoid(ny)
    if mask is not None:
        lk = jnp.where(mask, lk, 0.0)
    inc = _dot_split2(lk, m_inc2)
    w = jnp.exp2(_lane_tile(c, sub // LANES) + inc - ny)
    if mask is not None:
        w = jnp.where(mask, w, 0.0)
    acc_ref[...] += _dot(w.astype(BF16), vs)
    return c + jnp.broadcast_to(inc[:, 0:1], c.shape)


def _fox_block(q2, k_ref, v_ref, fk, start, sub, mask, m_ref, l_ref, acc_ref, cs):
    ks = k_ref[pl.ds(start, sub), :].astype(BF16)
    vs = v_ref[pl.ds(start, sub), :].astype(BF16)
    y = _dot_nt(q2, ks) * cs - fk
    if mask is not None:
        y = jnp.where(mask, y, NEG_INF)
    m_prev = m_ref[...]
    m_new = jnp.maximum(m_prev, jnp.max(y, axis=1, keepdims=True))
    alpha = jnp.exp2(m_prev - m_new)
    p = jnp.exp2(y - _lane_tile(m_new, sub // LANES))
    l_ref[...] = alpha * l_ref[...] + jnp.sum(p, axis=1, keepdims=True)
    acc_ref[...] = alpha * acc_ref[...] + _dot(p.astype(BF16), vs)
    m_ref[...] = m_new


def _prompt_attn_kernel(*refs, mode, g, tq, sub, cs):
    if mode == "sb":
        q_ref, k_ref, v_ref, o_ref, acc_ref = refs
    else:
        q_ref, k_ref, v_ref, *fk_refs, o_ref, acc_ref, m_ref, l_ref = refs
    i = pl.program_id(2)
    nd = tq // sub
    rows = g * tq
    q = q_ref[...]
    q2 = jnp.concatenate([q[:, h * HEAD_DIM:(h + 1) * HEAD_DIM] for h in range(g)], axis=0).astype(BF16)
    qi = lax.broadcasted_iota(jnp.int32, (rows, sub), 0) & (tq - 1)
    kj = lax.broadcasted_iota(jnp.int32, (rows, sub), 1)
    acc_ref[...] = jnp.zeros_like(acc_ref)

    if mode == "sb":
        m_inc = _tri(sub, lower_eq=True, stack=2)
        c = jnp.zeros((rows, LANES), F32)
        for d in reversed(range(nd)):
            start = pl.multiple_of(i * tq + d * sub, sub)
            c = _sb_block(q2, k_ref, v_ref, start, sub, kj + d * sub < qi, m_inc, c, acc_ref, cs)

        def body(jj, c):
            start = pl.multiple_of((i * nd - 1 - jj) * sub, sub)
            return _sb_block(q2, k_ref, v_ref, start, sub, None, m_inc, c, acc_ref, cs)

        lax.fori_loop(0, i * nd, body, c)
        out = acc_ref[...]
    else:
        m_ref[...] = jnp.full_like(m_ref, NEG_INF)
        l_ref[...] = jnp.zeros_like(l_ref)

        def fk_rows(start):
            return jnp.concatenate(
                [jnp.broadcast_to(r[:, pl.ds(start, sub)] * LOG2E, (tq, sub)) for r in fk_refs], axis=0)

        for d in range(nd):
            start = pl.multiple_of(i * tq + d * sub, sub)
            _fox_block(q2, k_ref, v_ref, fk_rows(start), start, sub, kj + d * sub <= qi,
                       m_ref, l_ref, acc_ref, cs)

        def body(jj, carry):
            start = pl.multiple_of(jj * sub, sub)
            _fox_block(q2, k_ref, v_ref, fk_rows(start), start, sub, None, m_ref, l_ref, acc_ref, cs)
            return carry

        lax.fori_loop(0, i * nd, body, 0)
        out = acc_ref[...] / l_ref[...]

    o_ref[...] = jnp.concatenate([out[h * tq:(h + 1) * tq] for h in range(g)], axis=1).astype(o_ref.dtype)


def _prompt_attn(q, k, v, frow, mode, layer, b, s, kv, g, qcol):
    assert s & (s - 1) == 0
    tq = min(512, s)
    sub = min(256, tq)
    nq = s // tq
    gw = g * HEAD_DIM
    kv_spec = pl.BlockSpec((None, s, HEAD_DIM), lambda bi, j, i: (layer, bi, j))
    in_specs = [pl.BlockSpec((tq, gw), lambda bi, j, i: (bi * nq + i, qcol // gw + j)), kv_spec, kv_spec]
    args = [q, k, v]
    scratch = [pltpu.VMEM((g * tq, HEAD_DIM), F32)]
    if mode == "fox":
        for h in range(g):
            in_specs.append(pl.BlockSpec((None, 1, s), lambda bi, j, i, h=h: (j * g + h, 0, bi)))
            args.append(frow)
        scratch += [pltpu.VMEM((g * tq, LANES), F32), pltpu.VMEM((g * tq, LANES), F32)]
    return pl.pallas_call(
        functools.partial(_prompt_attn_kernel, mode=mode, g=g, tq=tq, sub=sub, cs=HEAD_DIM ** -0.5 * LOG2E),
        grid=(b, kv, nq),
        in_specs=in_specs,
        out_specs=pl.BlockSpec((tq, gw), lambda bi, j, i: (bi * nq + i, j)),
        out_shape=jax.ShapeDtypeStruct((b * s, kv * gw), BF16),
        scratch_shapes=scratch,
        compiler_params=_params("parallel", "parallel", "parallel"),
        name="prompt_" + mode,
    )(*args)


def _even_sample_kernel(pt_ref, qa_ref, qb_ref, kan_ref, van_ref, kbn_ref, vbn_ref, lfn_ref,
                        ka_hbm, va_hbm, kb_hbm, vb_hbm, lf_hbm, oa_ref, ob_ref,
                        kbuf, lfbuf, sem, acca_ref, c_ref, accb_ref, m_ref, l_ref, rc_ref,
                        *, layer, npg, cp, nb, kv, g, tt, page, cs):
    b = pl.program_id(0)
    nc = npg // cp
    total = pl.num_programs(0) * nc
    nh = kv * g
    rows = nh * tt
    gr = g * tt
    prow = page * kv
    caches = (ka_hbm, va_hbm, kb_hbm, vb_hbm)
    m_inc = _tri(page, lower_eq=True)
    m_inc2 = _tri(page, lower_eq=True, stack=2)

    def chunk_copies(slot, phys):
        out = []
        for u in range(cp):
            pg = phys(u)
            for a, cache in enumerate(caches):
                out.append(pltpu.make_async_copy(cache.at[layer, pg], kbuf.at[slot, a, pl.ds(u * prow, prow)], sem.at[slot]))
            out.append(pltpu.make_async_copy(lf_hbm.at[layer, pg], lfbuf.at[slot, pl.ds(u * nh, nh)], sem.at[slot]))
        return out

    def start_chunk(gid):
        row = gid // nc
        base = row * npg + npg - (gid - row * nc + 1) * cp
        for c in chunk_copies(gid % nb, lambda u: pt_ref[base + u]):
            c.start()

    @pl.when(b == 0)
    def _():
        for n in range(nb - 1):
            start_chunk(jnp.int32(n))

    def stack_heads(q):
        return jnp.concatenate([q[:, h * HEAD_DIM:(h + 1) * HEAD_DIM] for h in range(nh)], axis=0).astype(BF16)

    def rep_rows(a):
        return jnp.concatenate([jnp.broadcast_to(a[h:h + 1, :], (tt, a.shape[1])) for h in range(nh)], axis=0)

    def scores(q, kblocks, c):
        return jnp.concatenate(
            [_dot_nt(q[k * gr:(k + 1) * gr], kblocks[k]) for k in range(kv)], axis=0) * c

    def weighted(w, vblocks):
        wb = w.astype(BF16)
        return jnp.concatenate(
            [_dot(wb[k * gr:(k + 1) * gr], vblocks[k]) for k in range(kv)], axis=0)

    qa = stack_heads(qa_ref[...])
    qb = stack_heads(qb_ref[...])

    zpad = jnp.zeros((page - tt, HEAD_DIM), F32)

    def new_blocks(ref):
        x = ref[...]
        return [jnp.concatenate([x[:, k * HEAD_DIM:(k + 1) * HEAD_DIM], zpad], axis=0).astype(BF16)
                for k in range(kv)]

    t_of = lax.broadcasted_iota(jnp.int32, (rows, page), 0) & (tt - 1)
    col = lax.broadcasted_iota(jnp.int32, (rows, page), 1)
    mask = col < t_of
    ny = scores(qa, new_blocks(kan_ref), -cs)
    inc = _dot_split2(jnp.where(mask, _log2_sigmoid(ny), 0.0), m_inc2)
    acca_ref[...] = weighted(jnp.where(mask, jnp.exp2(inc - ny), 0.0), new_blocks(van_ref))
    c_ref[...] = jnp.broadcast_to(inc[:, 0:1], c_ref.shape)
    fnew = _dot_split3(lfn_ref[...], _tri(page, lower_eq=False))
    yb = scores(qb, new_blocks(kbn_ref), cs) - rep_rows(fnew) * LOG2E
    yb = jnp.where(col <= t_of, yb, NEG_INF)
    mx = jnp.max(yb, axis=1, keepdims=True)
    p = jnp.exp2(yb - mx)
    m_ref[...] = jnp.broadcast_to(mx, m_ref.shape)
    l_ref[...] = jnp.broadcast_to(jnp.sum(p, axis=1, keepdims=True), l_ref.shape)
    accb_ref[...] = weighted(p, new_blocks(vbn_ref))
    rc_ref[...] = jnp.zeros_like(rc_ref)

    def chunk(cc, carry):
        gid = b * nc + cc
        nxt = gid + nb - 1

        @pl.when(nxt < total)
        def _():
            start_chunk(nxt)

        slot = gid % nb
        for c in chunk_copies(slot, lambda u: 0):
            c.wait()

        def blocks(a):
            view = kbuf.at[slot, a]
            return [view[pl.ds(k, cp * page, stride=kv), :].astype(BF16) for k in range(kv)]

        ny = scores(qa, blocks(0), -cs)
        lk = _log2_sigmoid(ny)
        inc = _dot_split2(jnp.concatenate([lk[:, u * page:(u + 1) * page] for u in range(cp)], axis=0), m_inc2)
        c = c_ref[...]
        ws = [None] * cp
        for u in reversed(range(cp)):
            inc_u = inc[u * rows:(u + 1) * rows]
            ws[u] = jnp.exp2(c + inc_u - ny[:, u * page:(u + 1) * page])
            c = c + jnp.broadcast_to(inc_u[:, 0:1], c.shape)
        c_ref[...] = c
        acca_ref[...] += weighted(jnp.concatenate(ws, axis=1), blocks(1))

        lf = lfbuf[slot]
        incf = _dot_split3(lf, m_inc)
        rc = rc_ref[...]
        bias = [None] * cp
        for u in reversed(range(cp)):
            incf_u = incf[u * nh:(u + 1) * nh]
            bias[u] = rep_rows(rc + incf_u - lf[u * nh:(u + 1) * nh])
            rc = rc + jnp.broadcast_to(incf_u[:, 0:1], rc.shape)
        rc_ref[...] = rc
        yb = scores(qb, blocks(2), cs) + jnp.concatenate(bias, axis=1) * LOG2E
        m_prev = m_ref[...]
        m_new = jnp.maximum(m_prev, jnp.max(yb, axis=1, keepdims=True))
        alpha = jnp.exp2(m_prev - m_new)
        p = jnp.exp2(yb - _lane_tile(m_new, cp))
        l_ref[...] = alpha * l_ref[...] + jnp.sum(p, axis=1, keepdims=True)
        accb_ref[...] = alpha * accb_ref[...] + weighted(p, blocks(3))
        m_ref[...] = m_new
        return carry

    lax.fori_loop(0, nc, chunk, 0)

    def unstack(a):
        return jnp.concatenate([a[h * tt:(h + 1) * tt] for h in range(nh)], axis=1)

    oa_ref[...] = unstack(acca_ref[...])
    ob_ref[...] = unstack(accb_ref[...] / l_ref[...])


def _even_sample(q, knew, lfn, caches, lf_cache, page_table, layer, kv, g, tt):
    db, npg = page_table.shape
    page = caches[0].shape[2] // kv
    nh = kv * g
    cp = PAGES_PER_CHUNK if npg % PAGES_PER_CHUNK == 0 else 1
    nb = CHUNK_SLOTS
    assert db * (npg // cp) >= nb - 1
    qw = nh * HEAD_DIM
    kw = kv * HEAD_DIM
    new_spec = pl.BlockSpec((None, tt, kw), lambda b, pt: (layer, b, 0))
    in_specs = [pl.BlockSpec((tt, qw), lambda b, pt: (b, 0)), pl.BlockSpec((tt, qw), lambda b, pt: (b, 1)),
                new_spec, new_spec, new_spec, new_spec,
                pl.BlockSpec((None, nh, LANES), lambda b, pt: (b, 0, 0))]
    in_specs += [pl.BlockSpec(memory_space=pl.ANY)] * 5
    rows = nh * tt
    return pl.pallas_call(
        functools.partial(_even_sample_kernel, layer=layer, npg=npg, cp=cp, nb=nb, kv=kv, g=g, tt=tt,
                          page=page, cs=HEAD_DIM ** -0.5 * LOG2E),
        grid_spec=pltpu.PrefetchScalarGridSpec(
            num_scalar_prefetch=1,
            grid=(db,),
            in_specs=in_specs,
            out_specs=[pl.BlockSpec((tt, qw), lambda b, pt: (b, 0)), pl.BlockSpec((tt, qw), lambda b, pt: (b, 0))],
            scratch_shapes=[pltpu.VMEM((nb, 4, cp * page * kv, HEAD_DIM), F32),
                            pltpu.VMEM((nb, cp * nh, page), F32),
                            pltpu.SemaphoreType.DMA((nb,)),
                            pltpu.VMEM((rows, HEAD_DIM), F32), pltpu.VMEM((rows, LANES), F32),
                            pltpu.VMEM((rows, HEAD_DIM), F32), pltpu.VMEM((rows, LANES), F32),
                            pltpu.VMEM((rows, LANES), F32), pltpu.VMEM((nh, LANES), F32)],
        ),
        out_shape=[jax.ShapeDtypeStruct((db * tt, qw), F32), jax.ShapeDtypeStruct((db * tt, qw), F32)],
        compiler_params=_params("arbitrary"),
        name="even_sample",
    )(page_table.reshape(-1), q, q, *knew, lfn, *caches, lf_cache)


def _out_proj_kernel(a1_ref, a2_ref, w1_ref, w2_ref, x_ref, o_ref):
    o_ref[...] = (x_ref[...] + _dot(a1_ref[...].astype(BF16), w1_ref[...])
                  + _dot(a2_ref[...].astype(BF16), w2_ref[...]))


def _out_proj(a1, a2, col2, w, x, row0):
    n = a1.shape[0]
    d = x.shape[1]
    kh = w.shape[0] // 2
    tm = _pick_tile(n, 512, LANES)
    assert row0 % tm == 0
    r0 = row0 // tm
    return pl.pallas_call(
        _out_proj_kernel,
        grid=(n // tm,),
        in_specs=[
            pl.BlockSpec((tm, kh), lambda i: (i, 0)),
            pl.BlockSpec((tm, kh), lambda i: (i, col2)),
            pl.BlockSpec((kh, d), lambda i: (0, 0)),
            pl.BlockSpec((kh, d), lambda i: (1, 0)),
            pl.BlockSpec((tm, d), lambda i: (r0 + i, 0)),
        ],
        out_specs=pl.BlockSpec((tm, d), lambda i: (r0 + i, 0)),
        out_shape=jax.ShapeDtypeStruct(x.shape, F32),
        input_output_aliases={4: 0},
        compiler_params=_params("parallel"),
        name="out_proj",
    )(a1, a2, w, w, x)


def _rope_table_kernel(pos_ref, inv_ref, cos_ref, sin_ref):
    ang = pos_ref[...] * inv_ref[...]
    lane = lax.broadcasted_iota(jnp.int32, ang.shape, 1)
    first = (lane & (HD_C - 1)) < HD_C // 2
    cos_ref[...] = jnp.cos(ang)
    sin_ref[...] = jnp.where(first, -1.0, 1.0) * jnp.sin(ang)


def _rope_table(pos, inv):
    t = pos.shape[0]
    tm = _pick_tile(t, 768, LANES)
    return pl.pallas_call(
        _rope_table_kernel,
        grid=(t // tm,),
        in_specs=[pl.BlockSpec((tm, LANES), lambda i: (i, 0)), pl.BlockSpec((1, LANES), lambda i: (0, 0))],
        out_specs=[pl.BlockSpec((tm, LANES), lambda i: (i, 0))] * 2,
        out_shape=[jax.ShapeDtypeStruct((t, LANES), F32)] * 2,
        compiler_params=_params("parallel"),
        name="rope_table",
    )(pos, inv)


def _proj_odd_kernel(x_ref, g_ref, w_ref, cos_ref, sin_ref, q_ref, kv_ref, h_ref, *, nq, rope_cols):
    j = pl.program_id(1)

    @pl.when(j == 0)
    def _():
        h_ref[...] = _rms(x_ref[...], g_ref[...]).astype(BF16)

    y = _dot(h_ref[...], w_ref[...])
    tn = y.shape[1]
    half = HD_C // 2
    lane = lax.broadcasted_iota(jnp.int32, y.shape, 1)
    first = (lane & (HD_C - 1)) < half
    partner = jnp.where(first, pltpu.roll(y, tn - half, 1), pltpu.roll(y, half, 1))
    reps = tn // LANES
    roped = y * _lane_tile(cos_ref[...], reps) + partner * _lane_tile(sin_ref[...], reps)
    out = jnp.where(lane + j * tn < rope_cols, roped, y)

    @pl.when(j < nq)
    def _():
        q_ref[...] = out

    @pl.when(j == nq)
    def _():
        kv_ref[...] = out


def _proj_odd(x, row0, n, g, w, cos_t, sin_t, qw, kwc):
    d = x.shape[1]
    tn = 2 * kwc
    assert qw % tn == 0 and w.shape[1] == qw + tn
    nq = qw // tn
    tm = _pick_tile(n, 1024, LANES)
    assert row0 % tm == 0
    r0 = row0 // tm
    return pl.pallas_call(
        functools.partial(_proj_odd_kernel, nq=nq, rope_cols=qw + kwc),
        grid=(n // tm, nq + 1),
        in_specs=[
            pl.BlockSpec((tm, d), lambda i, j: (r0 + i, 0)),
            pl.BlockSpec((1, d), lambda i, j: (0, 0)),
            pl.BlockSpec((d, tn), lambda i, j: (0, j)),
            pl.BlockSpec((tm, LANES), lambda i, j: (r0 + i, 0)),
            pl.BlockSpec((tm, LANES), lambda i, j: (r0 + i, 0)),
        ],
        out_specs=[pl.BlockSpec((tm, tn), lambda i, j: (i, jnp.minimum(j, nq - 1))),
                   pl.BlockSpec((tm, tn), lambda i, j: (i, 0))],
        out_shape=[jax.ShapeDtypeStruct((n, qw), F32), jax.ShapeDtypeStruct((n, tn), F32)],
        scratch_shapes=[pltpu.VMEM((tm, d), BF16)],
        compiler_params=_params("parallel", "arbitrary"),
        name="proj_odd",
    )(x, g.reshape(1, d), w, cos_t, sin_t)


def _swa_group(q, kband, vband, mask, sink_col, g, cs):
    tq = q.shape[0]
    q8 = jnp.concatenate([q[:, h * HD_C:(h + 1) * HD_C] for h in range(g)], axis=0).astype(BF16)
    y = _dot_nt(q8, kband.astype(BF16)) * cs
    y = jnp.where(mask, y, NEG_INF)
    m = jnp.maximum(jnp.max(y, axis=1, keepdims=True), sink_col)
    p = jnp.exp2(y - m)
    denom = jnp.sum(p, axis=1, keepdims=True) + jnp.exp2(sink_col - m)
    o = _dot(p.astype(BF16), vband.astype(BF16)) / denom
    return jnp.concatenate([o[h * tq:(h + 1) * tq] for h in range(g)], axis=1)


def _sink_col(sinks_ref, k, g, tq):
    return jnp.concatenate([jnp.full((tq, 1), sinks_ref[k * g + h] * LOG2E, F32) for h in range(g)], axis=0)


def _pair_diag(ref_p, ref_c, col, odd):
    x = jnp.concatenate([ref_p[:, col * LANES:(col + 1) * LANES], ref_c[:, col * LANES:(col + 1) * LANES]], axis=0)
    low = lax.broadcasted_iota(jnp.int32, x.shape, 1) < HD_C
    xr = pltpu.roll(x, HD_C, 1)
    top = jnp.where(low, xr if odd else x, 0.0)
    bot = jnp.where(low, 0.0, x if odd else xr)
    return jnp.concatenate([top, bot], axis=0)


def _swa_prompt_kernel(sinks_ref, q_ref, kvc_ref, kvp_ref, o_ref, *, kv, g, cs):
    i = pl.program_id(1)
    tq = q_ref.shape[0]
    nk = 2 * tq
    npair = g // 2
    cols = npair * tq
    kcols = kv * HD_C // LANES
    kj = lax.broadcasted_iota(jnp.int32, (2 * nk, cols), 0) & (nk - 1)
    qi = lax.broadcasted_iota(jnp.int32, (2 * nk, cols), 1) & (tq - 1)
    bias = jnp.where((kj > qi) & (kj <= qi + tq) & ((kj >= tq) | (i > 0)), 0.0, NEG_INF)
    first = lax.broadcasted_iota(jnp.int32, (LANES, cols), 0) < HD_C
    for k in range(kv):
        kd = _pair_diag(kvp_ref, kvc_ref, k // 2, k % 2).astype(BF16)
        vdt = _pair_diag(kvp_ref, kvc_ref, kcols + k // 2, k % 2).T.astype(BF16)
        q4 = jnp.concatenate([q_ref[:, (k * npair + j) * LANES:(k * npair + j + 1) * LANES] for j in range(npair)],
                             axis=0).astype(BF16)
        y = _dot_nt(kd, q4) * cs + bias
        ps, dens = [], []
        for h in range(2):
            yh = y[h * nk:(h + 1) * nk]
            sink = jnp.concatenate([jnp.full((1, tq), sinks_ref[k * g + 2 * j + h] * LOG2E, F32) for j in range(npair)], axis=1)
            m = jnp.maximum(jnp.max(yh, axis=0, keepdims=True), sink)
            p = jnp.exp2(yh - m)
            ps.append(p.astype(BF16))
            dens.append(jnp.sum(p, axis=0, keepdims=True) + jnp.exp2(sink - m))
        ot = _dot(vdt, jnp.concatenate(ps, axis=0)) / jnp.where(first, dens[0], dens[1])
        o = ot.T
        for j in range(npair):
            c0 = (k * npair + j) * LANES
            o_ref[:, c0:c0 + LANES] = o[j * tq:(j + 1) * tq].astype(o_ref.dtype)


def _swa_prompt(q, kvn, sinks, b, s, kv, g):
    nq = s // WINDOW
    qw = kv * g * HD_C
    kw2 = 2 * kv * HD_C
    return pl.pallas_call(
        functools.partial(_swa_prompt_kernel, kv=kv, g=g, cs=HD_C ** -0.5 * LOG2E),
        grid_spec=pltpu.PrefetchScalarGridSpec(
            num_scalar_prefetch=1,
            grid=(b, nq),
            in_specs=[pl.BlockSpec((WINDOW, qw), lambda bi, i, sk: (bi * nq + i, 0)),
                      pl.BlockSpec((WINDOW, kw2), lambda bi, i, sk: (bi * nq + i, 0)),
                      pl.BlockSpec((WINDOW, kw2), lambda bi, i, sk: (jnp.maximum(bi * nq + i - 1, 0), 0))],
            out_specs=pl.BlockSpec((WINDOW, qw), lambda bi, i, sk: (bi * nq + i, 0)),
        ),
        out_shape=jax.ShapeDtypeStruct((b * s, qw), BF16),
        compiler_params=_params("parallel", "parallel"),
        name="swa_prompt",
    )(sinks, q, kvn, kvn)


def _swa_sample_kernel(sinks_ref, q_ref, kvn_ref, kbuf_ref, vbuf_ref, o_ref, *, kv, g, tt, cs):
    gw = g * HD_C
    kw = kv * HD_C
    nk = 2 * WINDOW
    t_of = lax.broadcasted_iota(jnp.int32, (g * tt, nk), 0) & (tt - 1)
    col = lax.broadcasted_iota(jnp.int32, (g * tt, nk), 1)
    mask = ((col < WINDOW) & (col > t_of)) | ((col >= WINDOW) & (col - WINDOW <= t_of))
    zpad = jnp.zeros((WINDOW - tt, HD_C), F32)
    for k in range(kv):
        ks = slice(k * HD_C, (k + 1) * HD_C)
        vs = slice(kw + k * HD_C, kw + (k + 1) * HD_C)
        kband = jnp.concatenate([kbuf_ref[:, ks], kvn_ref[:, ks], zpad], axis=0)
        vband = jnp.concatenate([vbuf_ref[:, ks], kvn_ref[:, vs], zpad], axis=0)
        o = _swa_group(q_ref[:, k * gw:(k + 1) * gw], kband, vband, mask, _sink_col(sinks_ref, k, g, tt), g, cs)
        o_ref[:, k * gw:(k + 1) * gw] = o


def _swa_sample(q, kvn, sinks, kbuf, vbuf, layer, kv, g, tt):
    db = kbuf.shape[1]
    qw = kv * g * HD_C
    kw = kv * HD_C
    buf_spec = pl.BlockSpec((None, None, WINDOW, kw), lambda b, sk: (layer, b, 0, 0))
    return pl.pallas_call(
        functools.partial(_swa_sample_kernel, kv=kv, g=g, tt=tt, cs=HD_C ** -0.5 * LOG2E),
        grid_spec=pltpu.PrefetchScalarGridSpec(
            num_scalar_prefetch=1,
            grid=(db,),
            in_specs=[pl.BlockSpec((tt, qw), lambda b, sk: (b, 0)),
                      pl.BlockSpec((tt, 2 * kw), lambda b, sk: (b, 0)),
                      buf_spec, buf_spec],
            out_specs=pl.BlockSpec((tt, qw), lambda b, sk: (b, 0)),
        ),
        out_shape=jax.ShapeDtypeStruct((db * tt, qw), F32),
        compiler_params=_params("parallel"),
        name="swa_sample",
    )(sinks, q, kvn, kbuf, vbuf)


def _final_norm_kernel(x_ref, g_ref, o_ref):
    o_ref[...] = _rms(x_ref[...], g_ref[...])


def _final_norm(x, row0, n, g):
    d = x.shape[1]
    tm = _pick_tile(n, 1024, LANES)
    assert row0 % tm == 0
    r0 = row0 // tm
    return pl.pallas_call(
        _final_norm_kernel,
        grid=(n // tm,),
        in_specs=[pl.BlockSpec((tm, d), lambda i: (r0 + i, 0)), pl.BlockSpec((1, d), lambda i: (0, 0))],
        out_specs=pl.BlockSpec((tm, d), lambda i: (i, 0)),
        out_shape=jax.ShapeDtypeStruct((n, d), F32),
        compiler_params=_params("parallel"),
        name="final_norm",
    )(x, g.reshape(1, d))


def kernel(x_prompt, x_sample, cache_sb_k, cache_sb_v, cache_fox_k, cache_fox_v, cache_fox_logf, cache_swa_k, cache_swa_v, page_table, norm_ffn1, norm_mix, norm_ffn2, norm_final, w_ffn_in, w_ffn_out, w_in_even, b_forget, w_out_even, w_in_odd, sinks, w_out_odd):
    b, s, d = x_prompt.shape
    db, tt, _ = x_sample.shape
    depth = norm_mix.shape[0]
    n_even, n_phys, page, kv_a, _ = cache_sb_k.shape
    n_odd = cache_swa_k.shape[0]
    kv_c = cache_swa_k.shape[3]
    h_b = cache_fox_logf.shape[3]
    kv_b = cache_fox_k.shape[3]
    h_a = (w_in_even.shape[2] - h_b - 2 * (kv_a + kv_b) * HEAD_DIM) // HEAD_DIM - h_b
    g_a, g_b = h_a // kv_a, h_b // kv_b
    h_c = w_out_odd.shape[1] // HD_C
    g_c = h_c // kv_c
    assert kv_a == kv_b and g_a == g_b and h_b == SUBLANES and tt == SUBLANES
    bs = b * s
    ns = db * tt
    past_len = page_table.shape[1] * page
    kw = kv_a * HEAD_DIM
    qwa = h_a * HEAD_DIM
    qw = qwa + h_b * HEAD_DIM

    x = jnp.concatenate([x_prompt.reshape(bs, d), x_sample.reshape(ns, d)], axis=0)
    w_ffn_in = w_ffn_in.astype(BF16)
    w_ffn_out = w_ffn_out.astype(BF16)

    paged = [c.reshape(n_even, n_phys, page * kv_a, HEAD_DIM) for c in (cache_sb_k, cache_sb_v, cache_fox_k, cache_fox_v)]
    lf_cache = jnp.swapaxes(cache_fox_logf, 2, 3)
    swa_kbuf = cache_swa_k.reshape(n_odd, db, WINDOW, kv_c * HD_C)
    swa_vbuf = cache_swa_v.reshape(n_odd, db, WINDOW, kv_c * HD_C)

    half = HD_C // 2
    inv_freq = ROPE_THETA ** (-jnp.arange(half, dtype=F32) / half)
    inv = jnp.tile(inv_freq, LANES // half).reshape(1, LANES)
    pos = jnp.concatenate([jnp.tile(jnp.arange(s, dtype=jnp.int32), b),
                           jnp.tile(past_len + jnp.arange(tt, dtype=jnp.int32), db)]).astype(F32)
    cos_t, sin_t = _rope_table(jnp.broadcast_to(pos[:, None], (bs + ns, LANES)), inv)

    kv_p = [jnp.zeros((n_even, bs, kw), F32) for _ in range(4)]
    kv_s = [jnp.zeros((n_even, ns, kw), F32) for _ in range(4)]
    lf_p, lf_s = [], []
    odd_p = [[], []]
    odd_s = [[], []]

    for l in range(depth):
        x = _ffn(x, norm_ffn1[l], w_ffn_in[l, 0], w_ffn_out[l, 0])
        i = l // 2
        if l % 2 == 0:
            w_in = w_in_even[i]
            c0 = qwa + 2 * kw
            w_main = jnp.concatenate([w_in[:, :qwa], w_in[:, c0:c0 + qw - qwa], w_in[:, qwa:c0],
                                      w_in[:, c0 + qw - qwa:c0 + qw - qwa + 2 * kw]], axis=1).astype(BF16)
            w_f = jnp.pad(w_in[:, qw + 4 * kw:], ((0, 0), (0, LANES - h_b))).astype(BF16)
            b_f = jnp.pad(b_forget[i], (0, LANES - h_b)).reshape(1, LANES)
            q_p, kv_p, lfp, lft_p = _proj_even(x, 0, bs, norm_mix[l], w_main, w_f, b_f, qw, kw, i, n_even, kv_p)
            q_s, kv_s, lfs, lft_s = _proj_even(x, bs, ns, norm_mix[l], w_main, w_f, b_f, qw, kw, i, n_even, kv_s)
            lf_p.append(lfp[:, :h_b].reshape(b, s, h_b))
            lf_s.append(lfs[:, :h_b].reshape(db, tt, h_b))
            frow = _fcum(lft_p, b, s).reshape(h_b, 1, bs)
            oa_p = _prompt_attn(q_p, kv_p[0], kv_p[1], None, "sb", i, b, s, kv_a, g_a, 0)
            ob_p = _prompt_attn(q_p, kv_p[2], kv_p[3], frow, "fox", i, b, s, kv_b, g_b, qwa)
            lfn = jnp.pad(lft_s.reshape(h_b, db, tt).transpose(1, 0, 2), ((0, 0), (0, 0), (0, LANES - tt)))
            oa_s, ob_s = _even_sample(q_s, kv_s, lfn, paged, lf_cache, page_table, i, kv_a, g_a, tt)
            w_out = w_out_even[i].astype(BF16)
            x = _out_proj(oa_p, ob_p, 0, w_out, x, 0)
            x = _out_proj(oa_s, ob_s, 0, w_out, x, bs)
        else:
            qwc = h_c * HD_C
            kwc = kv_c * HD_C
            w_in = w_in_odd[i].astype(BF16)
            q_p, kvn_p = _proj_odd(x, 0, bs, norm_mix[l], w_in, cos_t, sin_t, qwc, kwc)
            q_s, kvn_s = _proj_odd(x, bs, ns, norm_mix[l], w_in, cos_t, sin_t, qwc, kwc)
            o_p = _swa_prompt(q_p, kvn_p, sinks[i], b, s, kv_c, g_c)
            o_s = _swa_sample(q_s, kvn_s, sinks[i], swa_kbuf, swa_vbuf, i, kv_c, g_c, tt)
            w_out = w_out_odd[i].astype(BF16)
            x = _out_proj(o_p, o_p, 1, w_out, x, 0)
            x = _out_proj(o_s, o_s, 1, w_out, x, bs)
            for n, buf in enumerate((cache_swa_k[i], cache_swa_v[i])):
                cols = slice(n * kwc, (n + 1) * kwc)
                odd_p[n].append(kvn_p.reshape(b, s, 2 * kwc)[:, s - WINDOW:, cols].reshape(b, WINDOW, kv_c, HD_C))
                odd_s[n].append(jnp.concatenate([buf[:, tt:], kvn_s[:, cols].reshape(db, tt, kv_c, HD_C)], axis=1))
        x = _ffn(x, norm_ffn2[l], w_ffn_in[l, 1], w_ffn_out[l, 1])

    outs = [_final_norm(x, 0, bs, norm_final).reshape(b, s, d), _final_norm(x, bs, ns, norm_final).reshape(db, tt, d)]
    outs += [a.reshape(n_even, b, s, kv_a, HEAD_DIM) for a in kv_p] + [jnp.stack(lf_p)]
    outs += [jnp.stack(a) for a in odd_p]
    outs += [a.reshape(n_even, db, tt, kv_a, HEAD_DIM) for a in kv_s] + [jnp.stack(lf_s)]
    outs += [jnp.stack(a) for a in odd_s]
    return tuple(outs)
```

```python
import functools
import math

import jax
import jax.numpy as jnp
from jax import lax
from jax.experimental import pallas as pl
from jax.experimental.pallas import tpu as pltpu

F32 = jnp.float32
BF16 = jnp.bfloat16

HEAD_DIM = 128
HD_C = 64
WINDOW = 128
ROPE_THETA = 10000.0
EPS = 1e-6
NEG_INF = -1e30
LOG2E = math.log2(math.e)
LANES = 128
SUBLANES = 8
VMEM_LIMIT = 56 * 1024 * 1024
PAGES_PER_CHUNK = 8
CHUNK_SLOTS = 3


def _pick_tile(n, target, mult):
    best = None
    for t in range(mult, min(n, target) + 1, mult):
        if n % t == 0:
            best = t
    assert best is not None, (n, target, mult)
    return best


def _params(*sem):
    return pltpu.CompilerParams(dimension_semantics=sem, vmem_limit_bytes=VMEM_LIMIT)


def _dot(a, b):
    return jnp.dot(a, b, preferred_element_type=F32)


def _dot_nt(a, b):
    return lax.dot_general(a, b, (((1,), (1,)), ((), ())), preferred_element_type=F32)


def _rms(x, g):
    ms = jnp.mean(x * x, axis=-1, keepdims=True)
    return x * lax.rsqrt(ms + EPS) * g


def _log_sigmoid(x):
    return jnp.minimum(x, 0.0) - jnp.log1p(jnp.exp(-jnp.abs(x)))


def _log2_sigmoid(ny):
    neg_abs = lax.bitcast_convert_type(lax.bitcast_convert_type(ny, jnp.uint32) | jnp.uint32(0x80000000), F32)
    return jnp.minimum(ny, 0.0) - jnp.log2(1.0 + jnp.exp2(neg_abs))


def _split2(a):
    hi = a.astype(BF16)
    lo = (a - hi.astype(F32)).astype(BF16)
    return hi, lo


def _dot_split2(a, m01x2):
    return _dot(jnp.concatenate(_split2(a), axis=1), m01x2)


def _dot_split3(a, m01):
    a1 = a.astype(BF16)
    r1 = a - a1.astype(F32)
    a2 = r1.astype(BF16)
    a3 = (r1 - a2.astype(F32)).astype(BF16)
    return _dot(a1, m01) + _dot(a2, m01) + _dot(a3, m01)


def _tri(n, lower_eq, stack=1):
    r = lax.broadcasted_iota(jnp.int32, (stack * n, n), 0) & (n - 1)
    c = lax.broadcasted_iota(jnp.int32, (stack * n, n), 1)
    m = (r >= c) if lower_eq else (r <= c)
    return jnp.where(m, 1.0, 0.0).astype(BF16)


def _lane_tile(a, reps):
    return a if reps == 1 else jnp.concatenate([a] * reps, axis=1)


def _ffn_kernel(x_ref, g_ref, wg_ref, wu_ref, wo_ref, *rest, convert_next):
    if convert_next:
        nin_ref, nout_ref, o_ref, cin_ref, cout_ref, h_ref = rest
        cin_ref[...] = nin_ref[...].astype(BF16)
        cout_ref[...] = nout_ref[...].astype(BF16)
    else:
        o_ref, h_ref = rest

    @pl.when(pl.program_id(1) == 0)
    def _():
        x = x_ref[...]
        h_ref[...] = _rms(x, g_ref[...]).astype(BF16)
        o_ref[...] = x

    h = h_ref[...]
    gate = _dot(h, wg_ref[...])
    up = _dot(h, wu_ref[...])
    a = (gate * jax.nn.sigmoid(gate)) * (0.5 * up)
    o_ref[...] += _dot(a.astype(BF16), wo_ref[...])


def _pow2_blocks(n, limit, mult):
    k = 1
    while 2 * k <= limit and n % (2 * k) == 0 and (n // (2 * k)) % mult == 0:
        k *= 2
    return k


def _ffn(x, g, w_in, w_out, nxt=None):
    t, d = x.shape
    f = w_out.shape[0]
    tm = _pick_tile(t, 768, LANES)
    tf = _pick_tile(f, 512, LANES)
    nf = f // tf
    ni = t // tm
    in_specs = [
        pl.BlockSpec((tm, d), lambda i, j: (i, 0)),
        pl.BlockSpec((1, d), lambda i, j: (0, 0)),
        pl.BlockSpec((d, tf), lambda i, j: (0, j)),
        pl.BlockSpec((d, tf), lambda i, j: (0, nf + j)),
        pl.BlockSpec((tf, d), lambda i, j: (j, 0)),
    ]
    args = [x, g.reshape(1, d), w_in, w_in, w_out]
    out_specs = [pl.BlockSpec((tm, d), lambda i, j: (i, 0))]
    out_shape = [jax.ShapeDtypeStruct((t, d), F32)]
    if nxt is not None:
        w_in_all, w_out_all, ln, hn = nxt
        nr = _pow2_blocks(d, ni, 2 * SUBLANES)
        ncol = _pow2_blocks(d, ni, LANES)
        rb, cb = d // nr, 2 * f // nf
        rb2, cb2 = f // nf, d // ncol

        def in_idx(i, j):
            return jnp.minimum(i, nr - 1), jnp.where(i < nr, j, nf - 1)

        def out_idx(i, j):
            return jnp.where(i < ncol, j, nf - 1), jnp.minimum(i, ncol - 1)

        in_specs += [pl.BlockSpec((None, None, rb, cb), lambda i, j: (ln, hn, *in_idx(i, j))),
                     pl.BlockSpec((None, None, rb2, cb2), lambda i, j: (ln, hn, *out_idx(i, j)))]
        args += [w_in_all, w_out_all]
        out_specs += [pl.BlockSpec((rb, cb), in_idx), pl.BlockSpec((rb2, cb2), out_idx)]
        out_shape += [jax.ShapeDtypeStruct((d, 2 * f), BF16), jax.ShapeDtypeStruct((f, d), BF16)]
    outs = pl.pallas_call(
        functools.partial(_ffn_kernel, convert_next=nxt is not None),
        grid=(ni, nf),
        in_specs=in_specs,
        out_specs=out_specs,
        out_shape=out_shape,
        scratch_shapes=[pltpu.VMEM((tm, d), BF16)],
        compiler_params=_params("arbitrary", "arbitrary"),
        name="ffn",
    )(*args)
    return outs if nxt is not None else (outs[0], None, None)


def _proj_even_kernel(*refs, nq, nkv):
    x_ref, g_ref, w_ref, wf_ref, bf_ref = refs[:5]
    outs = refs[5 + nkv:]
    q_ref, kv_refs, lf_ref, lft_ref, h_ref = outs[0], outs[1:1 + nkv], outs[1 + nkv], outs[2 + nkv], outs[3 + nkv]
    j = pl.program_id(1)

    @pl.when(j == 0)
    def _():
        hb = _rms(x_ref[...], g_ref[...]).astype(BF16)
        h_ref[...] = hb
        lf = _log_sigmoid(_dot(hb, wf_ref[...]) + bf_ref[...])
        lf_ref[...] = lf
        lft_ref[...] = lf.T[:SUBLANES, :]

    y = _dot(h_ref[...], w_ref[...])

    @pl.when(j < nq)
    def _():
        q_ref[...] = y

    for n, ref in enumerate(kv_refs):
        @pl.when(j == nq + n)
        def _(ref=ref):
            ref[...] = y


def _proj_even(x, row0, n, g, w_main, w_f, b_f, qw, kw, layer, n_layers, prev):
    d = x.shape[1]
    tm = _pick_tile(n, 1024, LANES)
    assert row0 % tm == 0 and qw % kw == 0
    r0 = row0 // tm
    nq = qw // kw
    nkv = (w_main.shape[1] - qw) // kw
    assert len(prev) == nkv
    in_specs = [
        pl.BlockSpec((tm, d), lambda i, j: (r0 + i, 0)),
        pl.BlockSpec((1, d), lambda i, j: (0, 0)),
        pl.BlockSpec((d, kw), lambda i, j: (0, j)),
        pl.BlockSpec((d, LANES), lambda i, j: (0, 0)),
        pl.BlockSpec((1, LANES), lambda i, j: (0, 0)),
    ]
    args = [x, g.reshape(1, d), w_main, w_f, b_f]
    aliases = {}
    for m, arr in enumerate(prev):
        in_specs.append(pl.BlockSpec(memory_space=pl.ANY))
        args.append(arr)
        aliases[5 + m] = 1 + m
    out_specs = [pl.BlockSpec((tm, kw), lambda i, j: (i, jnp.minimum(j, nq - 1)))]
    out_shape = [jax.ShapeDtypeStruct((n, qw), F32)]
    for _ in range(nkv):
        out_specs.append(pl.BlockSpec((None, tm, kw), lambda i, j: (layer, i, 0)))
        out_shape.append(jax.ShapeDtypeStruct((n_layers, n, kw), F32))
    out_specs += [pl.BlockSpec((tm, LANES), lambda i, j: (i, 0)), pl.BlockSpec((SUBLANES, tm), lambda i, j: (0, i))]
    out_shape += [jax.ShapeDtypeStruct((n, LANES), F32), jax.ShapeDtypeStruct((SUBLANES, n), F32)]
    outs = pl.pallas_call(
        functools.partial(_proj_even_kernel, nq=nq, nkv=nkv),
        grid=(n // tm, nq + nkv),
        in_specs=in_specs,
        out_specs=out_specs,
        out_shape=out_shape,
        scratch_shapes=[pltpu.VMEM((tm, d), BF16)],
        input_output_aliases=aliases,
        compiler_params=_params("parallel", "arbitrary"),
        name="proj_even",
    )(*args)
    return outs[0], list(outs[1:1 + nkv]), outs[1 + nkv], outs[2 + nkv]


def _fcum_kernel(lft_ref, o_ref, carry_ref):
    @pl.when(pl.program_id(1) == 0)
    def _():
        carry_ref[...] = jnp.zeros_like(carry_ref)

    blk = lft_ref.shape[1]
    cum = _dot_split3(lft_ref[...], _tri(blk, lower_eq=False)) + carry_ref[:, 0:1]
    o_ref[...] = cum
    carry_ref[...] = jnp.broadcast_to(cum[:, blk - 1:blk], carry_ref.shape)


def _fcum(lft, b, s):
    blk = _pick_tile(s, 512, LANES)
    nb = s // blk
    return pl.pallas_call(
        _fcum_kernel,
        grid=(b, nb),
        in_specs=[pl.BlockSpec((SUBLANES, blk), lambda i, j: (0, i * nb + j))],
        out_specs=pl.BlockSpec((SUBLANES, blk), lambda i, j: (0, i * nb + j)),
        out_shape=jax.ShapeDtypeStruct((SUBLANES, b * s), F32),
        scratch_shapes=[pltpu.VMEM((SUBLANES, LANES), F32)],
        compiler_params=_params("parallel", "arbitrary"),
        name="fcum",
    )(lft)


def _block_scores(q2, k_ref, start, sub, c):
    if not isinstance(start, int):
        start = pl.multiple_of(start, sub)
    return _dot_nt(q2, k_ref[pl.ds(start, sub), :].astype(BF16)) * c


def _sb_block(ny, v_ref, start, sub, mask, m_inc2, c, acc_ref):
    vs = v_ref[pl.ds(pl.multiple_of(start, sub), sub), :].astype(BF16)
    lk = _log2_sigmoid(ny)
    if mask is not None:
        lk = jnp.where(mask, lk, 0.0)
    inc = _dot_split2(lk, m_inc2)
    w = jnp.exp2(_lane_tile(c, sub // LANES) + inc - ny)
    if mask is not None:
        w = jnp.where(mask, w, 0.0)
    acc_ref[...] += _dot(w.astype(BF16), vs)
    return c + jnp.broadcast_to(inc[:, 0:1], c.shape)


def _fox_block(y, v_ref, fk, start, sub, mask, m_ref, l_ref, acc_ref):
    vs = v_ref[pl.ds(pl.multiple_of(start, sub), sub), :].astype(BF16)
    y = y - fk
    if mask is not None:
        y = jnp.where(mask, y, NEG_INF)
    m_prev = m_ref[...]
    m_new = jnp.maximum(m_prev, jnp.max(y, axis=1, keepdims=True))
    alpha = jnp.exp2(m_prev - m_new)
    p = jnp.exp2(y - _lane_tile(m_new, sub // LANES))
    l_ref[...] = alpha * l_ref[...] + jnp.sum(p, axis=1, keepdims=True)
    acc_ref[...] = alpha * acc_ref[...] + _dot(p.astype(BF16), vs)
    m_ref[...] = m_new


def _prompt_attn_kernel(*refs, mode, g, tq, sub, cs):
    if mode == "sb":
        q_ref, k_ref, v_ref, o_ref, acc_ref = refs
    else:
        q_ref, k_ref, v_ref, *fk_refs, o_ref, acc_ref, m_ref, l_ref = refs
    i = pl.program_id(2)
    nd = tq // sub
    rows = g * tq
    q = q_ref[...]
    q2 = jnp.concatenate([q[:, h * HEAD_DIM:(h + 1) * HEAD_DIM] for h in range(g)], axis=0).astype(BF16)
    qi = lax.broadcasted_iota(jnp.int32, (rows, sub), 0) & (tq - 1)
    kj = lax.broadcasted_iota(jnp.int32, (rows, sub), 1)
    acc_ref[...] = jnp.zeros_like(acc_ref)

    n_loop = i * nd
    if mode == "sb":
        m_inc = _tri(sub, lower_eq=True, stack=2)
        c = jnp.zeros((rows, LANES), F32)
        ny = _block_scores(q2, k_ref, i * tq + (nd - 1) * sub, sub, -cs)
        for d in reversed(range(nd)):
            start = i * tq + d * sub
            ny_next = _block_scores(q2, k_ref, jnp.maximum(start - sub, 0), sub, -cs)
            c = _sb_block(ny, v_ref, start, sub, kj + d * sub < qi, m_inc, c, acc_ref)
            ny = ny_next

        def body(jb, carry):
            c, ny = carry
            for d in reversed(range(nd)):
                start = (i - 1 - jb) * tq + d * sub
                ny_next = _block_scores(q2, k_ref, jnp.maximum(start - sub, 0), sub, -cs)
                c = _sb_block(ny, v_ref, start, sub, None, m_inc, c, acc_ref)
                ny = ny_next
            return c, ny

        lax.fori_loop(0, i, body, (c, ny))
        out = acc_ref[...]
    else:
        m_ref[...] = jnp.full_like(m_ref, NEG_INF)
        l_ref[...] = jnp.zeros_like(l_ref)

        def fk_rows(start):
            return jnp.concatenate(
                [jnp.broadcast_to(r[:, pl.ds(pl.multiple_of(start, sub), sub)] * LOG2E, (tq, sub)) for r in fk_refs],
                axis=0)

        for d in range(nd):
            start = i * tq + d * sub
            _fox_block(_block_scores(q2, k_ref, start, sub, cs), v_ref, fk_rows(start), start, sub,
                       kj + d * sub <= qi, m_ref, l_ref, acc_ref)

        def body(jj, carry):
            start = jj * sub
            _fox_block(_block_scores(q2, k_ref, start, sub, cs), v_ref, fk_rows(start), start, sub,
                       None, m_ref, l_ref, acc_ref)
            return carry

        lax.fori_loop(0, n_loop, body, 0)
        out = acc_ref[...] / l_ref[...]

    o_ref[...] = jnp.concatenate([out[h * tq:(h + 1) * tq] for h in range(g)], axis=1).astype(o_ref.dtype)


def _prompt_attn(q, k, v, frow, mode, layer, b, s, kv, g, qcol):
    assert s & (s - 1) == 0
    tq = min(512, s)
    sub = min(256, tq)
    nq = s // tq
    gw = g * HEAD_DIM
    kv_spec = pl.BlockSpec((None, s, HEAD_DIM), lambda bi, j, i: (layer, bi, j))
    in_specs = [pl.BlockSpec((tq, gw), lambda bi, j, i: (bi * nq + i, qcol // gw + j)), kv_spec, kv_spec]
    args = [q, k, v]
    scratch = [pltpu.VMEM((g * tq, HEAD_DIM), F32)]
    if mode == "fox":
        for h in range(g):
            in_specs.append(pl.BlockSpec((None, 1, s), lambda bi, j, i, h=h: (j * g + h, 0, bi)))
            args.append(frow)
        scratch += [pltpu.VMEM((g * tq, LANES), F32), pltpu.VMEM((g * tq, LANES), F32)]
    return pl.pallas_call(
        functools.partial(_prompt_attn_kernel, mode=mode, g=g, tq=tq, sub=sub, cs=HEAD_DIM ** -0.5 * LOG2E),
        grid=(b, kv, nq),
        in_specs=in_specs,
        out_specs=pl.BlockSpec((tq, gw), lambda bi, j, i: (bi * nq + i, j)),
        out_shape=jax.ShapeDtypeStruct((b * s, kv * gw), BF16),
        scratch_shapes=scratch,
        compiler_params=_params("parallel", "parallel", "parallel"),
        name="prompt_" + mode,
    )(*args)


def _even_sample_kernel(pt_ref, qa_ref, qb_ref, kan_ref, van_ref, kbn_ref, vbn_ref, lfn_ref,
                        ka_hbm, va_hbm, kb_hbm, vb_hbm, lf_hbm, oa_ref, ob_ref,
                        kbuf, lfbuf, sem, acca_ref, c_ref, accb_ref, m_ref, l_ref, rc_ref,
                        *, layer, npg, cp, nb, kv, g, tt, page, cs):
    b = pl.program_id(0)
    nc = npg // cp
    total = pl.num_programs(0) * nc
    nh = kv * g
    rows = nh * tt
    gr = g * tt
    prow = page * kv
    caches = (ka_hbm, va_hbm, kb_hbm, vb_hbm)
    m_inc = _tri(page, lower_eq=True)
    m_inc2 = _tri(page, lower_eq=True, stack=2)

    def chunk_copies(slot, phys):
        out = []
        for u in range(cp):
            pg = phys(u)
            for a, cache in enumerate(caches):
                out.append(pltpu.make_async_copy(cache.at[layer, pg], kbuf.at[slot, a, pl.ds(u * prow, prow)], sem.at[slot]))
            out.append(pltpu.make_async_copy(lf_hbm.at[layer, pg], lfbuf.at[slot, pl.ds(u * nh, nh)], sem.at[slot]))
        return out

    def start_chunk(gid):
        row = gid // nc
        base = row * npg + npg - (gid - row * nc + 1) * cp
        for c in chunk_copies(gid % nb, lambda u: pt_ref[base + u]):
            c.start()

    @pl.when(b == 0)
    def _():
        for n in range(nb - 1):
            start_chunk(jnp.int32(n))

    def stack_heads(q):
        return jnp.concatenate([q[:, h * HEAD_DIM:(h + 1) * HEAD_DIM] for h in range(nh)], axis=0).astype(BF16)

    def rep_rows(a):
        return jnp.concatenate([jnp.broadcast_to(a[h:h + 1, :], (tt, a.shape[1])) for h in range(nh)], axis=0)

    def scores(q, kblocks, c):
        return jnp.concatenate(
            [_dot_nt(q[k * gr:(k + 1) * gr], kblocks[k]) for k in range(kv)], axis=0) * c

    def weighted(w, vblocks):
        wb = w.astype(BF16)
        return jnp.concatenate(
            [_dot(wb[k * gr:(k + 1) * gr], vblocks[k]) for k in range(kv)], axis=0)

    qa = stack_heads(qa_ref[...])
    qb = stack_heads(qb_ref[...])

    zpad = jnp.zeros((page - tt, HEAD_DIM), F32)

    def new_blocks(ref):
        x = ref[...]
        return [jnp.concatenate([x[:, k * HEAD_DIM:(k + 1) * HEAD_DIM], zpad], axis=0).astype(BF16)
                for k in range(kv)]

    t_of = lax.broadcasted_iota(jnp.int32, (rows, page), 0) & (tt - 1)
    col = lax.broadcasted_iota(jnp.int32, (rows, page), 1)
    mask = col < t_of
    ny = scores(qa, new_blocks(kan_ref), -cs)
    inc = _dot_split2(jnp.where(mask, _log2_sigmoid(ny), 0.0), m_inc2)
    acca_ref[...] = weighted(jnp.where(mask, jnp.exp2(inc - ny), 0.0), new_blocks(van_ref))
    c_ref[...] = jnp.broadcast_to(inc[:, 0:1], c_ref.shape)
    fnew = _dot_split3(lfn_ref[...], _tri(page, lower_eq=False))
    yb = scores(qb, new_blocks(kbn_ref), cs) - rep_rows(fnew) * LOG2E
    yb = jnp.where(col <= t_of, yb, NEG_INF)
    mx = jnp.max(yb, axis=1, keepdims=True)
    p = jnp.exp2(yb - mx)
    m_ref[...] = jnp.broadcast_to(mx, m_ref.shape)
    l_ref[...] = jnp.broadcast_to(jnp.sum(p, axis=1, keepdims=True), l_ref.shape)
    accb_ref[...] = weighted(p, new_blocks(vbn_ref))
    rc_ref[...] = jnp.zeros_like(rc_ref)

    def chunk(cc, carry):
        gid = b * nc + cc
        nxt = gid + nb - 1

        @pl.when(nxt < total)
        def _():
            start_chunk(nxt)

        slot = gid % nb
        for c in chunk_copies(slot, lambda u: 0):
            c.wait()

        def blocks(a):
            view = kbuf.at[slot, a]
            return [view[pl.ds(k, cp * page, stride=kv), :].astype(BF16) for k in range(kv)]

        ny = scores(qa, blocks(0), -cs)
        lk = _log2_sigmoid(ny)
        inc = _dot_split2(jnp.concatenate([lk[:, u * page:(u + 1) * page] for u in range(cp)], axis=0), m_inc2)
        c = c_ref[...]
        ws = [None] * cp
        for u in reversed(range(cp)):
            inc_u = inc[u * rows:(u + 1) * rows]
            ws[u] = jnp.exp2(c + inc_u - ny[:, u * page:(u + 1) * page])
            c = c + jnp.broadcast_to(inc_u[:, 0:1], c.shape)
        c_ref[...] = c
        acca_ref[...] += weighted(jnp.concatenate(ws, axis=1), blocks(1))

        lf = lfbuf[slot]
        incf = _dot_split3(lf, m_inc)
        rc = rc_ref[...]
        bias = [None] * cp
        for u in reversed(range(cp)):
            incf_u = incf[u * nh:(u + 1) * nh]
            bias[u] = rep_rows(rc + incf_u - lf[u * nh:(u + 1) * nh])
            rc = rc + jnp.broadcast_to(incf_u[:, 0:1], rc.shape)
        rc_ref[...] = rc
        yb = scores(qb, blocks(2), cs) + jnp.concatenate(bias, axis=1) * LOG2E
        m_prev = m_ref[...]
        m_new = jnp.maximum(m_prev, jnp.max(yb, axis=1, keepdims=True))
        alpha = jnp.exp2(m_prev - m_new)
        p = jnp.exp2(yb - _lane_tile(m_new, cp))
        l_ref[...] = alpha * l_ref[...] + jnp.sum(p, axis=1, keepdims=True)
        accb_ref[...] = alpha * accb_ref[...] + weighted(p, blocks(3))
        m_ref[...] = m_new
        return carry

    lax.fori_loop(0, nc, chunk, 0)

    def unstack(a):
        return jnp.concatenate([a[h * tt:(h + 1) * tt] for h in range(nh)], axis=1)

    oa_ref[...] = unstack(acca_ref[...])
    ob_ref[...] = unstack(accb_ref[...] / l_ref[...])


def _even_sample(q, knew, lfn, caches, lf_cache, page_table, layer, kv, g, tt):
    db, npg = page_table.shape
    page = caches[0].shape[2] // kv
    nh = kv * g
    cp = PAGES_PER_CHUNK if npg % PAGES_PER_CHUNK == 0 else 1
    nb = CHUNK_SLOTS
    assert db * (npg // cp) >= nb - 1
    qw = nh * HEAD_DIM
    kw = kv * HEAD_DIM
    new_spec = pl.BlockSpec((None, tt, kw), lambda b, pt: (layer, b, 0))
    in_specs = [pl.BlockSpec((tt, qw), lambda b, pt: (b, 0)), pl.BlockSpec((tt, qw), lambda b, pt: (b, 1)),
                new_spec, new_spec, new_spec, new_spec,
                pl.BlockSpec((None, nh, LANES), lambda b, pt: (b, 0, 0))]
    in_specs += [pl.BlockSpec(memory_space=pl.ANY)] * 5
    rows = nh * tt
    return pl.pallas_call(
        functools.partial(_even_sample_kernel, layer=layer, npg=npg, cp=cp, nb=nb, kv=kv, g=g, tt=tt,
                          page=page, cs=HEAD_DIM ** -0.5 * LOG2E),
        grid_spec=pltpu.PrefetchScalarGridSpec(
            num_scalar_prefetch=1,
            grid=(db,),
            in_specs=in_specs,
            out_specs=[pl.BlockSpec((tt, qw), lambda b, pt: (b, 0)), pl.BlockSpec((tt, qw), lambda b, pt: (b, 0))],
            scratch_shapes=[pltpu.VMEM((nb, 4, cp * page * kv, HEAD_DIM), F32),
                            pltpu.VMEM((nb, cp * nh, page), F32),
                            pltpu.SemaphoreType.DMA((nb,)),
                            pltpu.VMEM((rows, HEAD_DIM), F32), pltpu.VMEM((rows, LANES), F32),
                            pltpu.VMEM((rows, HEAD_DIM), F32), pltpu.VMEM((rows, LANES), F32),
                            pltpu.VMEM((rows, LANES), F32), pltpu.VMEM((nh, LANES), F32)],
        ),
        out_shape=[jax.ShapeDtypeStruct((db * tt, qw), F32), jax.ShapeDtypeStruct((db * tt, qw), F32)],
        compiler_params=_params("arbitrary"),
        name="even_sample",
    )(page_table.reshape(-1), q, q, *knew, lfn, *caches, lf_cache)


def _out_proj_kernel(a1_ref, a2_ref, w1_ref, w2_ref, x_ref, o_ref):
    o_ref[...] = (x_ref[...] + _dot(a1_ref[...].astype(BF16), w1_ref[...])
                  + _dot(a2_ref[...].astype(BF16), w2_ref[...]))


def _out_proj(a1, a2, col2, w, x, row0):
    n = a1.shape[0]
    d = x.shape[1]
    kh = w.shape[0] // 2
    tm = _pick_tile(n, 512, LANES)
    assert row0 % tm == 0
    r0 = row0 // tm
    return pl.pallas_call(
        _out_proj_kernel,
        grid=(n // tm,),
        in_specs=[
            pl.BlockSpec((tm, kh), lambda i: (i, 0)),
            pl.BlockSpec((tm, kh), lambda i: (i, col2)),
            pl.BlockSpec((kh, d), lambda i: (0, 0)),
            pl.BlockSpec((kh, d), lambda i: (1, 0)),
            pl.BlockSpec((tm, d), lambda i: (r0 + i, 0)),
        ],
        out_specs=pl.BlockSpec((tm, d), lambda i: (r0 + i, 0)),
        out_shape=jax.ShapeDtypeStruct(x.shape, F32),
        input_output_aliases={4: 0},
        compiler_params=_params("parallel"),
        name="out_proj",
    )(a1, a2, w, w, x)


def _rope_table_kernel(pos_ref, inv_ref, cos_ref, sin_ref):
    ang = pos_ref[...] * inv_ref[...]
    lane = lax.broadcasted_iota(jnp.int32, ang.shape, 1)
    first = (lane & (HD_C - 1)) < HD_C // 2
    cos_ref[...] = jnp.cos(ang)
    sin_ref[...] = jnp.where(first, -1.0, 1.0) * jnp.sin(ang)


def _rope_table(pos, inv):
    t = pos.shape[0]
    tm = _pick_tile(t, 768, LANES)
    return pl.pallas_call(
        _rope_table_kernel,
        grid=(t // tm,),
        in_specs=[pl.BlockSpec((tm, LANES), lambda i: (i, 0)), pl.BlockSpec((1, LANES), lambda i: (0, 0))],
        out_specs=[pl.BlockSpec((tm, LANES), lambda i: (i, 0))] * 2,
        out_shape=[jax.ShapeDtypeStruct((t, LANES), F32)] * 2,
        compiler_params=_params("parallel"),
        name="rope_table",
    )(pos, inv)


def _proj_odd_kernel(x_ref, g_ref, w_ref, cos_ref, sin_ref, q_ref, kv_ref, h_ref, *, nq, rope_cols):
    j = pl.program_id(1)

    @pl.when(j == 0)
    def _():
        h_ref[...] = _rms(x_ref[...], g_ref[...]).astype(BF16)

    y = _dot(h_ref[...], w_ref[...])
    tn = y.shape[1]
    half = HD_C // 2
    lane = lax.broadcasted_iota(jnp.int32, y.shape, 1)
    first = (lane & (HD_C - 1)) < half
    partner = jnp.where(first, pltpu.roll(y, tn - half, 1), pltpu.roll(y, half, 1))
    reps = tn // LANES
    roped = y * _lane_tile(cos_ref[...], reps) + partner * _lane_tile(sin_ref[...], reps)
    out = jnp.where(lane + j * tn < rope_cols, roped, y)

    @pl.when(j < nq)
    def _():
        q_ref[...] = out

    @pl.when(j == nq)
    def _():
        kv_ref[...] = out


def _proj_odd(x, row0, n, g, w, cos_t, sin_t, qw, kwc):
    d = x.shape[1]
    tn = 2 * kwc
    assert qw % tn == 0 and w.shape[1] == qw + tn
    nq = qw // tn
    tm = _pick_tile(n, 1024, LANES)
    assert row0 % tm == 0
    r0 = row0 // tm
    return pl.pallas_call(
        functools.partial(_proj_odd_kernel, nq=nq, rope_cols=qw + kwc),
        grid=(n // tm, nq + 1),
        in_specs=[
            pl.BlockSpec((tm, d), lambda i, j: (r0 + i, 0)),
            pl.BlockSpec((1, d), lambda i, j: (0, 0)),
            pl.BlockSpec((d, tn), lambda i, j: (0, j)),
            pl.BlockSpec((tm, LANES), lambda i, j: (r0 + i, 0)),
            pl.BlockSpec((tm, LANES), lambda i, j: (r0 + i, 0)),
        ],
        out_specs=[pl.BlockSpec((tm, tn), lambda i, j: (i, jnp.minimum(j, nq - 1))),
                   pl.BlockSpec((tm, tn), lambda i, j: (i, 0))],
        out_shape=[jax.ShapeDtypeStruct((n, qw), F32), jax.ShapeDtypeStruct((n, tn), F32)],
        scratch_shapes=[pltpu.VMEM((tm, d), BF16)],
        compiler_params=_params("parallel", "arbitrary"),
        name="proj_odd",
    )(x, g.reshape(1, d), w, cos_t, sin_t)


def _swa_group(q, kband, vband, mask, sink_col, g, cs):
    tq = q.shape[0]
    q8 = jnp.concatenate([q[:, h * HD_C:(h + 1) * HD_C] for h in range(g)], axis=0).astype(BF16)
    y = _dot_nt(q8, kband.astype(BF16)) * cs
    y = jnp.where(mask, y, NEG_INF)
    m = jnp.maximum(jnp.max(y, axis=1, keepdims=True), sink_col)
    p = jnp.exp2(y - m)
    denom = jnp.sum(p, axis=1, keepdims=True) + jnp.exp2(sink_col - m)
    o = _dot(p.astype(BF16), vband.astype(BF16)) / denom
    return jnp.concatenate([o[h * tq:(h + 1) * tq] for h in range(g)], axis=1)


def _sink_col(sinks_ref, k, g, tq):
    return jnp.concatenate([jnp.full((tq, 1), sinks_ref[k * g + h] * LOG2E, F32) for h in range(g)], axis=0)


def _pair_diag(ref_p, ref_c, col, odd):
    x = jnp.concatenate([ref_p[:, col * LANES:(col + 1) * LANES], ref_c[:, col * LANES:(col + 1) * LANES]], axis=0)
    low = lax.broadcasted_iota(jnp.int32, x.shape, 1) < HD_C
    xr = pltpu.roll(x, HD_C, 1)
    top = jnp.where(low, xr if odd else x, 0.0)
    bot = jnp.where(low, 0.0, x if odd else xr)
    return jnp.concatenate([top, bot], axis=0)


def _swa_prompt_kernel(sinks_ref, q_ref, kvc_ref, kvp_ref, o_ref, *, kv, g, cs):
    i = pl.program_id(1)
    tq = q_ref.shape[0]
    nk = 2 * tq
    npair = g // 2
    cols = npair * tq
    kcols = kv * HD_C // LANES
    kj = lax.broadcasted_iota(jnp.int32, (2 * nk, cols), 0) & (nk - 1)
    qi = lax.broadcasted_iota(jnp.int32, (2 * nk, cols), 1) & (tq - 1)
    bias = jnp.where((kj > qi) & (kj <= qi + tq) & ((kj >= tq) | (i > 0)), 0.0, NEG_INF)
    first = lax.broadcasted_iota(jnp.int32, (LANES, cols), 0) < HD_C
    for k in range(kv):
        kd = _pair_diag(kvp_ref, kvc_ref, k // 2, k % 2).astype(BF16)
        vdt = _pair_diag(kvp_ref, kvc_ref, kcols + k // 2, k % 2).T.astype(BF16)
        q4 = jnp.concatenate([q_ref[:, (k * npair + j) * LANES:(k * npair + j + 1) * LANES] for j in range(npair)],
                             axis=0).astype(BF16)
        y = _dot_nt(kd, q4) * cs + bias
        ps, dens = [], []
        for h in range(2):
            yh = y[h * nk:(h + 1) * nk]
            sink = jnp.concatenate([jnp.full((1, tq), sinks_ref[k * g + 2 * j + h] * LOG2E, F32) for j in range(npair)], axis=1)
            m = jnp.maximum(jnp.max(yh, axis=0, keepdims=True), sink)
            p = jnp.exp2(yh - m)
            ps.append(p.astype(BF16))
            dens.append(jnp.sum(p, axis=0, keepdims=True) + jnp.exp2(sink - m))
        ot = _dot(vdt, jnp.concatenate(ps, axis=0)) / jnp.where(first, dens[0], dens[1])
        o = ot.T
        for j in range(npair):
            c0 = (k * npair + j) * LANES
            o_ref[:, c0:c0 + LANES] = o[j * tq:(j + 1) * tq].astype(o_ref.dtype)


def _swa_prompt(q, kvn, sinks, b, s, kv, g):
    nq = s // WINDOW
    qw = kv * g * HD_C
    kw2 = 2 * kv * HD_C
    return pl.pallas_call(
        functools.partial(_swa_prompt_kernel, kv=kv, g=g, cs=HD_C ** -0.5 * LOG2E),
        grid_spec=pltpu.PrefetchScalarGridSpec(
            num_scalar_prefetch=1,
            grid=(b, nq),
            in_specs=[pl.BlockSpec((WINDOW, qw), lambda bi, i, sk: (bi * nq + i, 0)),
                      pl.BlockSpec((WINDOW, kw2), lambda bi, i, sk: (bi * nq + i, 0)),
                      pl.BlockSpec((WINDOW, kw2), lambda bi, i, sk: (jnp.maximum(bi * nq + i - 1, 0), 0))],
            out_specs=pl.BlockSpec((WINDOW, qw), lambda bi, i, sk: (bi * nq + i, 0)),
        ),
        out_shape=jax.ShapeDtypeStruct((b * s, qw), BF16),
        compiler_params=_params("parallel", "parallel"),
        name="swa_prompt",
    )(sinks, q, kvn, kvn)


def _swa_sample_kernel(sinks_ref, q_ref, kvn_ref, kbuf_ref, vbuf_ref, o_ref, *, kv, g, tt, cs):
    gw = g * HD_C
    kw = kv * HD_C
    nk = 2 * WINDOW
    t_of = lax.broadcasted_iota(jnp.int32, (g * tt, nk), 0) & (tt - 1)
    col = lax.broadcasted_iota(jnp.int32, (g * tt, nk), 1)
    mask = ((col < WINDOW) & (col > t_of)) | ((col >= WINDOW) & (col - WINDOW <= t_of))
    zpad = jnp.zeros((WINDOW - tt, HD_C), F32)
    for k in range(kv):
        ks = slice(k * HD_C, (k + 1) * HD_C)
        vs = slice(kw + k * HD_C, kw + (k + 1) * HD_C)
        kband = jnp.concatenate([kbuf_ref[:, ks], kvn_ref[:, ks], zpad], axis=0)
        vband = jnp.concatenate([vbuf_ref[:, ks], kvn_ref[:, vs], zpad], axis=0)
        o = _swa_group(q_ref[:, k * gw:(k + 1) * gw], kband, vband, mask, _sink_col(sinks_ref, k, g, tt), g, cs)
        o_ref[:, k * gw:(k + 1) * gw] = o


def _swa_sample(q, kvn, sinks, kbuf, vbuf, layer, kv, g, tt):
    db = kbuf.shape[1]
    qw = kv * g * HD_C
    kw = kv * HD_C
    buf_spec = pl.BlockSpec((None, None, WINDOW, kw), lambda b, sk: (layer, b, 0, 0))
    return pl.pallas_call(
        functools.partial(_swa_sample_kernel, kv=kv, g=g, tt=tt, cs=HD_C ** -0.5 * LOG2E),
        grid_spec=pltpu.PrefetchScalarGridSpec(
            num_scalar_prefetch=1,
            grid=(db,),
            in_specs=[pl.BlockSpec((tt, qw), lambda b, sk: (b, 0)),
                      pl.BlockSpec((tt, 2 * kw), lambda b, sk: (b, 0)),
                      buf_spec, buf_spec],
            out_specs=pl.BlockSpec((tt, qw), lambda b, sk: (b, 0)),
        ),
        out_shape=jax.ShapeDtypeStruct((db * tt, qw), F32),
        compiler_params=_params("parallel"),
        name="swa_sample",
    )(sinks, q, kvn, kbuf, vbuf)


def _final_norm_kernel(x_ref, g_ref, o_ref):
    o_ref[...] = _rms(x_ref[...], g_ref[...])


def _final_norm(x, row0, n, g):
    d = x.shape[1]
    tm = _pick_tile(n, 1024, LANES)
    assert row0 % tm == 0
    r0 = row0 // tm
    return pl.pallas_call(
        _final_norm_kernel,
        grid=(n // tm,),
        in_specs=[pl.BlockSpec((tm, d), lambda i: (r0 + i, 0)), pl.BlockSpec((1, d), lambda i: (0, 0))],
        out_specs=pl.BlockSpec((tm, d), lambda i: (i, 0)),
        out_shape=jax.ShapeDtypeStruct((n, d), F32),
        compiler_params=_params("parallel"),
        name="final_norm",
    )(x, g.reshape(1, d))


def kernel(x_prompt, x_sample, cache_sb_k, cache_sb_v, cache_fox_k, cache_fox_v, cache_fox_logf, cache_swa_k, cache_swa_v, page_table, norm_ffn1, norm_mix, norm_ffn2, norm_final, w_ffn_in, w_ffn_out, w_in_even, b_forget, w_out_even, w_in_odd, sinks, w_out_odd):
    b, s, d = x_prompt.shape
    db, tt, _ = x_sample.shape
    depth = norm_mix.shape[0]
    n_even, n_phys, page, kv_a, _ = cache_sb_k.shape
    n_odd = cache_swa_k.shape[0]
    kv_c = cache_swa_k.shape[3]
    h_b = cache_fox_logf.shape[3]
    kv_b = cache_fox_k.shape[3]
    h_a = (w_in_even.shape[2] - h_b - 2 * (kv_a + kv_b) * HEAD_DIM) // HEAD_DIM - h_b
    g_a, g_b = h_a // kv_a, h_b // kv_b
    h_c = w_out_odd.shape[1] // HD_C
    g_c = h_c // kv_c
    assert kv_a == kv_b and g_a == g_b and h_b == SUBLANES and tt == SUBLANES
    bs = b * s
    ns = db * tt
    past_len = page_table.shape[1] * page
    kw = kv_a * HEAD_DIM
    qwa = h_a * HEAD_DIM
    qw = qwa + h_b * HEAD_DIM

    x = jnp.concatenate([x_prompt.reshape(bs, d), x_sample.reshape(ns, d)], axis=0)
    w_in_b = w_ffn_in[0, 0].astype(BF16)
    w_out_b = w_ffn_out[0, 0].astype(BF16)

    paged = [c.reshape(n_even, n_phys, page * kv_a, HEAD_DIM) for c in (cache_sb_k, cache_sb_v, cache_fox_k, cache_fox_v)]
    lf_cache = jnp.swapaxes(cache_fox_logf, 2, 3)
    swa_kbuf = cache_swa_k.reshape(n_odd, db, WINDOW, kv_c * HD_C)
    swa_vbuf = cache_swa_v.reshape(n_odd, db, WINDOW, kv_c * HD_C)

    half = HD_C // 2
    inv_freq = ROPE_THETA ** (-jnp.arange(half, dtype=F32) / half)
    inv = jnp.tile(inv_freq, LANES // half).reshape(1, LANES)
    pos = jnp.concatenate([jnp.tile(jnp.arange(s, dtype=jnp.int32), b),
                           jnp.tile(past_len + jnp.arange(tt, dtype=jnp.int32), db)]).astype(F32)
    cos_t, sin_t = _rope_table(jnp.broadcast_to(pos[:, None], (bs + ns, LANES)), inv)

    kv_p = [jnp.zeros((n_even, bs, kw), F32) for _ in range(4)]
    kv_s = [jnp.zeros((n_even, ns, kw), F32) for _ in range(4)]
    lf_p, lf_s = [], []
    odd_p = [[], []]
    odd_s = [[], []]

    for l in range(depth):
        x, w_in_b, w_out_b = _ffn(x, norm_ffn1[l], w_in_b, w_out_b, (w_ffn_in, w_ffn_out, l, 1))
        i = l // 2
        if l % 2 == 0:
            w_in = w_in_even[i]
            c0 = qwa + 2 * kw
            w_main = jnp.concatenate([w_in[:, :qwa], w_in[:, c0:c0 + qw - qwa], w_in[:, qwa:c0],
                                      w_in[:, c0 + qw - qwa:c0 + qw - qwa + 2 * kw]], axis=1).astype(BF16)
            w_f = jnp.pad(w_in[:, qw + 4 * kw:], ((0, 0), (0, LANES - h_b))).astype(BF16)
            b_f = jnp.pad(b_forget[i], (0, LANES - h_b)).reshape(1, LANES)
            q_p, kv_p, lfp, lft_p = _proj_even(x, 0, bs, norm_mix[l], w_main, w_f, b_f, qw, kw, i, n_even, kv_p)
            q_s, kv_s, lfs, lft_s = _proj_even(x, bs, ns, norm_mix[l], w_main, w_f, b_f, qw, kw, i, n_even, kv_s)
            lf_p.append(lfp[:, :h_b].reshape(b, s, h_b))
            lf_s.append(lfs[:, :h_b].reshape(db, tt, h_b))
            frow = _fcum(lft_p, b, s).reshape(h_b, 1, bs)
            oa_p = _prompt_attn(q_p, kv_p[0], kv_p[1], None, "sb", i, b, s, kv_a, g_a, 0)
            ob_p = _prompt_attn(q_p, kv_p[2], kv_p[3], frow, "fox", i, b, s, kv_b, g_b, qwa)
            lfn = jnp.pad(lft_s.reshape(h_b, db, tt).transpose(1, 0, 2), ((0, 0), (0, 0), (0, LANES - tt)))
            oa_s, ob_s = _even_sample(q_s, kv_s, lfn, paged, lf_cache, page_table, i, kv_a, g_a, tt)
            w_out = w_out_even[i].astype(BF16)
            x = _out_proj(oa_p, ob_p, 0, w_out, x, 0)
            x = _out_proj(oa_s, ob_s, 0, w_out, x, bs)
        else:
            qwc = h_c * HD_C
            kwc = kv_c * HD_C
            w_in = w_in_odd[i].astype(BF16)
            q_p, kvn_p = _proj_odd(x, 0, bs, norm_mix[l], w_in, cos_t, sin_t, qwc, kwc)
            q_s, kvn_s = _proj_odd(x, bs, ns, norm_mix[l], w_in, cos_t, sin_t, qwc, kwc)
            o_p = _swa_prompt(q_p, kvn_p, sinks[i], b, s, kv_c, g_c)
            o_s = _swa_sample(q_s, kvn_s, sinks[i], swa_kbuf, swa_vbuf, i, kv_c, g_c, tt)
            w_out = w_out_odd[i].astype(BF16)
            x = _out_proj(o_p, o_p, 1, w_out, x, 0)
            x = _out_proj(o_s, o_s, 1, w_out, x, bs)
            for n, buf in enumerate((cache_swa_k[i], cache_swa_v[i])):
                cols = slice(n * kwc, (n + 1) * kwc)
                odd_p[n].append(kvn_p.reshape(b, s, 2 * kwc)[:, s - WINDOW:, cols].reshape(b, WINDOW, kv_c, HD_C))
                odd_s[n].append(jnp.concatenate([buf[:, tt:], kvn_s[:, cols].reshape(db, tt, kv_c, HD_C)], axis=1))
        x, w_in_b, w_out_b = _ffn(x, norm_ffn2[l], w_in_b, w_out_b,
                                  (w_ffn_in, w_ffn_out, l + 1, 0) if l + 1 < depth else None)

    outs = [_final_norm(x, 0, bs, norm_final).reshape(b, s, d), _final_norm(x, bs, ns, norm_final).reshape(db, tt, d)]
    outs += [a.reshape(n_even, b, s, kv_a, HEAD_DIM) for a in kv_p] + [jnp.stack(lf_p)]
    outs += [jnp.stack(a) for a in odd_p]
    outs += [a.reshape(n_even, db, tt, kv_a, HEAD_DIM) for a in kv_s] + [jnp.stack(lf_s)]
    outs += [jnp.stack(a) for a in odd_s]
    return tuple(outs)
```

```python
import functools
import math

import jax
import jax.numpy as jnp
from jax import lax
from jax.experimental import pallas as pl
from jax.experimental.pallas import tpu as pltpu

F32 = jnp.float32
BF16 = jnp.bfloat16

HEAD_DIM = 128
HD_C = 64
WINDOW = 128
ROPE_THETA = 10000.0
EPS = 1e-6
NEG_INF = -1e30
LOG2E = math.log2(math.e)
LANES = 128
SUBLANES = 8
VMEM_LIMIT = 56 * 1024 * 1024
PAGES_PER_CHUNK = 8
CHUNK_SLOTS = 3


def _pick_tile(n, target, mult):
    best = None
    for t in range(mult, min(n, target) + 1, mult):
        if n % t == 0:
            best = t
    assert best is not None, (n, target, mult)
    return best


def _params(*sem):
    return pltpu.CompilerParams(dimension_semantics=sem, vmem_limit_bytes=VMEM_LIMIT)


def _dot(a, b):
    return jnp.dot(a, b, preferred_element_type=F32)


def _dot_nt(a, b):
    return lax.dot_general(a, b, (((1,), (1,)), ((), ())), preferred_element_type=F32)


def _rms(x, g):
    ms = jnp.mean(x * x, axis=-1, keepdims=True)
    return x * lax.rsqrt(ms + EPS) * g


def _log_sigmoid(x):
    return jnp.minimum(x, 0.0) - jnp.log1p(jnp.exp(-jnp.abs(x)))


def _log2_sigmoid(ny):
    neg_abs = lax.bitcast_convert_type(lax.bitcast_convert_type(ny, jnp.uint32) | jnp.uint32(0x80000000), F32)
    return jnp.minimum(ny, 0.0) - jnp.log2(1.0 + jnp.exp2(neg_abs))


def _split2(a):
    hi = a.astype(BF16)
    lo = (a - hi.astype(F32)).astype(BF16)
    return hi, lo


def _dot_split2(a, m01x2):
    return _dot(jnp.concatenate(_split2(a), axis=1), m01x2)


def _dot_split3(a, m01):
    a1 = a.astype(BF16)
    r1 = a - a1.astype(F32)
    a2 = r1.astype(BF16)
    a3 = (r1 - a2.astype(F32)).astype(BF16)
    return _dot(a1, m01) + _dot(a2, m01) + _dot(a3, m01)


def _tri(n, lower_eq, stack=1):
    r = lax.broadcasted_iota(jnp.int32, (stack * n, n), 0) & (n - 1)
    c = lax.broadcasted_iota(jnp.int32, (stack * n, n), 1)
    m = (r >= c) if lower_eq else (r <= c)
    return jnp.where(m, 1.0, 0.0).astype(BF16)


def _lane_tile(a, reps):
    return a if reps == 1 else jnp.concatenate([a] * reps, axis=1)


def _ffn_kernel(x_ref, g_ref, wg_ref, wu_ref, wo_ref, *rest, convert_next):
    if convert_next:
        nin_ref, nout_ref, o_ref, cin_ref, cout_ref, h_ref = rest
        cin_ref[...] = nin_ref[...].astype(BF16)
        cout_ref[...] = nout_ref[...].astype(BF16)
    else:
        o_ref, h_ref = rest

    @pl.when(pl.program_id(1) == 0)
    def _():
        x = x_ref[...]
        h_ref[...] = _rms(x, g_ref[...]).astype(BF16)
        o_ref[...] = x

    h = h_ref[...]
    gate = _dot(h, wg_ref[...])
    up = _dot(h, wu_ref[...])
    a = (gate * jax.nn.sigmoid(gate)) * (0.5 * up)
    o_ref[...] += _dot(a.astype(BF16), wo_ref[...])


def _pow2_blocks(n, limit, mult):
    k = 1
    while 2 * k <= limit and n % (2 * k) == 0 and (n // (2 * k)) % mult == 0:
        k *= 2
    return k


def _ffn(x, g, w_in, w_out, nxt=None):
    t, d = x.shape
    f = w_out.shape[0]
    tm = _pick_tile(t, 768, LANES)
    tf = _pick_tile(f, 512, LANES)
    nf = f // tf
    ni = t // tm
    in_specs = [
        pl.BlockSpec((tm, d), lambda i, j: (i, 0)),
        pl.BlockSpec((1, d), lambda i, j: (0, 0)),
        pl.BlockSpec((d, tf), lambda i, j: (0, j)),
        pl.BlockSpec((d, tf), lambda i, j: (0, nf + j)),
        pl.BlockSpec((tf, d), lambda i, j: (j, 0)),
    ]
    args = [x, g.reshape(1, d), w_in, w_in, w_out]
    out_specs = [pl.BlockSpec((tm, d), lambda i, j: (i, 0))]
    out_shape = [jax.ShapeDtypeStruct((t, d), F32)]
    if nxt is not None:
        w_in_all, w_out_all, ln, hn = nxt
        nr = _pow2_blocks(d, ni, 2 * SUBLANES)
        ncol = _pow2_blocks(d, ni, LANES)
        rb, cb = d // nr, 2 * f // nf
        rb2, cb2 = f // nf, d // ncol

        def in_idx(i, j):
            return jnp.minimum(i, nr - 1), jnp.where(i < nr, j, nf - 1)

        def out_idx(i, j):
            return jnp.where(i < ncol, j, nf - 1), jnp.minimum(i, ncol - 1)

        in_specs += [pl.BlockSpec((None, None, rb, cb), lambda i, j: (ln, hn, *in_idx(i, j))),
                     pl.BlockSpec((None, None, rb2, cb2), lambda i, j: (ln, hn, *out_idx(i, j)))]
        args += [w_in_all, w_out_all]
        out_specs += [pl.BlockSpec((rb, cb), in_idx), pl.BlockSpec((rb2, cb2), out_idx)]
        out_shape += [jax.ShapeDtypeStruct((d, 2 * f), BF16), jax.ShapeDtypeStruct((f, d), BF16)]
    outs = pl.pallas_call(
        functools.partial(_ffn_kernel, convert_next=nxt is not None),
        grid=(ni, nf),
        in_specs=in_specs,
        out_specs=out_specs,
        out_shape=out_shape,
        scratch_shapes=[pltpu.VMEM((tm, d), BF16)],
        compiler_params=_params("arbitrary", "arbitrary"),
        name="ffn",
    )(*args)
    return outs if nxt is not None else (outs[0], None, None)


def _proj_even_kernel(*refs, nq, nkv):
    x_ref, g_ref, w_ref, wf_ref, bf_ref = refs[:5]
    outs = refs[5 + nkv:]
    q_ref, kv_refs, lf_ref, lft_ref, h_ref = outs[0], outs[1:1 + nkv], outs[1 + nkv], outs[2 + nkv], outs[3 + nkv]
    j = pl.program_id(1)

    @pl.when(j == 0)
    def _():
        hb = _rms(x_ref[...], g_ref[...]).astype(BF16)
        h_ref[...] = hb
        lf = _log_sigmoid(_dot(hb, wf_ref[...]) + bf_ref[...])
        lf_ref[...] = lf
        lft_ref[...] = lf.T[:SUBLANES, :]

    y = _dot(h_ref[...], w_ref[...].astype(BF16))

    @pl.when(j < nq)
    def _():
        q_ref[...] = y

    for n, ref in enumerate(kv_refs):
        @pl.when(j == nq + n)
        def _(ref=ref):
            ref[...] = y


def _proj_even(x, row0, n, g, w_all, w_f, b_f, qwa, qw, kw, layer, n_layers, prev):
    d = x.shape[1]
    tm = _pick_tile(n, 1024, LANES)
    assert row0 % tm == 0 and qw % kw == 0 and qwa % kw == 0
    r0 = row0 // tm
    nq = qw // kw
    nqa = qwa // kw
    nkv = len(prev)

    def wcol(j):
        return jnp.where(j < nqa, j, jnp.where(j < nq, j + 2, jnp.where(j < nq + 2, j - (nq - nqa), j)))

    in_specs = [
        pl.BlockSpec((tm, d), lambda i, j: (r0 + i, 0)),
        pl.BlockSpec((1, d), lambda i, j: (0, 0)),
        pl.BlockSpec((None, d, kw), lambda i, j: (layer, 0, wcol(j))),
        pl.BlockSpec((d, LANES), lambda i, j: (0, 0)),
        pl.BlockSpec((1, LANES), lambda i, j: (0, 0)),
    ]
    args = [x, g.reshape(1, d), w_all, w_f, b_f]
    aliases = {}
    for m, arr in enumerate(prev):
        in_specs.append(pl.BlockSpec(memory_space=pl.ANY))
        args.append(arr)
        aliases[5 + m] = 1 + m
    out_specs = [pl.BlockSpec((tm, kw), lambda i, j: (i, jnp.minimum(j, nq - 1)))]
    out_shape = [jax.ShapeDtypeStruct((n, qw), F32)]
    for _ in range(nkv):
        out_specs.append(pl.BlockSpec((None, tm, kw), lambda i, j: (layer, i, 0)))
        out_shape.append(jax.ShapeDtypeStruct((n_layers, n, kw), F32))
    out_specs += [pl.BlockSpec((tm, LANES), lambda i, j: (i, 0)), pl.BlockSpec((SUBLANES, tm), lambda i, j: (0, i))]
    out_shape += [jax.ShapeDtypeStruct((n, LANES), F32), jax.ShapeDtypeStruct((SUBLANES, n), F32)]
    outs = pl.pallas_call(
        functools.partial(_proj_even_kernel, nq=nq, nkv=nkv),
        grid=(n // tm, nq + nkv),
        in_specs=in_specs,
        out_specs=out_specs,
        out_shape=out_shape,
        scratch_shapes=[pltpu.VMEM((tm, d), BF16)],
        input_output_aliases=aliases,
        compiler_params=_params("parallel", "arbitrary"),
        name="proj_even",
    )(*args)
    return outs[0], list(outs[1:1 + nkv]), outs[1 + nkv], outs[2 + nkv]


def _fcum_kernel(lft_ref, o_ref, carry_ref):
    @pl.when(pl.program_id(1) == 0)
    def _():
        carry_ref[...] = jnp.zeros_like(carry_ref)

    blk = lft_ref.shape[1]
    cum = _dot_split3(lft_ref[...], _tri(blk, lower_eq=False)) + carry_ref[:, 0:1]
    o_ref[...] = cum
    carry_ref[...] = jnp.broadcast_to(cum[:, blk - 1:blk], carry_ref.shape)


def _fcum(lft, b, s):
    blk = _pick_tile(s, 512, LANES)
    nb = s // blk
    return pl.pallas_call(
        _fcum_kernel,
        grid=(b, nb),
        in_specs=[pl.BlockSpec((SUBLANES, blk), lambda i, j: (0, i * nb + j))],
        out_specs=pl.BlockSpec((SUBLANES, blk), lambda i, j: (0, i * nb + j)),
        out_shape=jax.ShapeDtypeStruct((SUBLANES, b * s), F32),
        scratch_shapes=[pltpu.VMEM((SUBLANES, LANES), F32)],
        compiler_params=_params("parallel", "arbitrary"),
        name="fcum",
    )(lft)


def _block_scores(q2, k_ref, start, sub, c):
    if not isinstance(start, int):
        start = pl.multiple_of(start, sub)
    return _dot_nt(q2, k_ref[pl.ds(start, sub), :].astype(BF16)) * c


def _sb_block(ny, v_ref, start, sub, mask, m_inc2, c, acc_ref):
    vs = v_ref[pl.ds(pl.multiple_of(start, sub), sub), :].astype(BF16)
    lk = _log2_sigmoid(ny)
    if mask is not None:
        lk = jnp.where(mask, lk, 0.0)
    inc = _dot_split2(lk, m_inc2)
    w = jnp.exp2(_lane_tile(c, sub // LANES) + inc - ny)
    if mask is not None:
        w = jnp.where(mask, w, 0.0)
    acc_ref[...] += _dot(w.astype(BF16), vs)
    return c + jnp.broadcast_to(inc[:, 0:1], c.shape)


def _fox_block(y, v_ref, fk, start, sub, mask, m_ref, l_ref, acc_ref):
    vs = v_ref[pl.ds(pl.multiple_of(start, sub), sub), :].astype(BF16)
    y = y - fk
    if mask is not None:
        y = jnp.where(mask, y, NEG_INF)
    m_prev = m_ref[...]
    m_new = jnp.maximum(m_prev, jnp.max(y, axis=1, keepdims=True))
    alpha = jnp.exp2(m_prev - m_new)
    p = jnp.exp2(y - _lane_tile(m_new, sub // LANES))
    l_ref[...] = alpha * l_ref[...] + jnp.sum(p, axis=1, keepdims=True)
    acc_ref[...] = alpha * acc_ref[...] + _dot(p.astype(BF16), vs)
    m_ref[...] = m_new


def _prompt_attn_kernel(*refs, mode, g, tq, sub, cs):
    if mode == "sb":
        q_ref, k_ref, v_ref, o_ref, acc_ref = refs
    else:
        q_ref, k_ref, v_ref, *fk_refs, o_ref, acc_ref, m_ref, l_ref = refs
    i = pl.program_id(2)
    nd = tq // sub
    rows = g * tq
    q = q_ref[...]
    q2 = jnp.concatenate([q[:, h * HEAD_DIM:(h + 1) * HEAD_DIM] for h in range(g)], axis=0).astype(BF16)
    qi = lax.broadcasted_iota(jnp.int32, (rows, sub), 0) & (tq - 1)
    kj = lax.broadcasted_iota(jnp.int32, (rows, sub), 1)
    acc_ref[...] = jnp.zeros_like(acc_ref)

    n_loop = i * nd
    if mode == "sb":
        m_inc = _tri(sub, lower_eq=True, stack=2)
        c = jnp.zeros((rows, LANES), F32)
        ny = _block_scores(q2, k_ref, i * tq + (nd - 1) * sub, sub, -cs)
        for d in reversed(range(nd)):
            start = i * tq + d * sub
            ny_next = _block_scores(q2, k_ref, jnp.maximum(start - sub, 0), sub, -cs)
            c = _sb_block(ny, v_ref, start, sub, kj + d * sub < qi, m_inc, c, acc_ref)
            ny = ny_next

        def body(jb, carry):
            c, ny = carry
            for d in reversed(range(nd)):
                start = (i - 1 - jb) * tq + d * sub
                ny_next = _block_scores(q2, k_ref, jnp.maximum(start - sub, 0), sub, -cs)
                c = _sb_block(ny, v_ref, start, sub, None, m_inc, c, acc_ref)
                ny = ny_next
            return c, ny

        lax.fori_loop(0, i, body, (c, ny))
        out = acc_ref[...]
    else:
        m_ref[...] = jnp.full_like(m_ref, NEG_INF)
        l_ref[...] = jnp.zeros_like(l_ref)

        def fk_rows(start):
            return jnp.concatenate(
                [jnp.broadcast_to(r[:, pl.ds(pl.multiple_of(start, sub), sub)] * LOG2E, (tq, sub)) for r in fk_refs],
                axis=0)

        for d in range(nd):
            start = i * tq + d * sub
            _fox_block(_block_scores(q2, k_ref, start, sub, cs), v_ref, fk_rows(start), start, sub,
                       kj + d * sub <= qi, m_ref, l_ref, acc_ref)

        def body(jj, carry):
            start = jj * sub
            _fox_block(_block_scores(q2, k_ref, start, sub, cs), v_ref, fk_rows(start), start, sub,
                       None, m_ref, l_ref, acc_ref)
            return carry

        lax.fori_loop(0, n_loop, body, 0)
        out = acc_ref[...] / l_ref[...]

    o_ref[...] = jnp.concatenate([out[h * tq:(h + 1) * tq] for h in range(g)], axis=1).astype(o_ref.dtype)


def _prompt_attn(q, k, v, frow, mode, layer, b, s, kv, g, qcol):
    assert s & (s - 1) == 0
    tq = min(512, s)
    sub = min(256 if mode == "sb" else 512, tq)
    nq = s // tq
    gw = g * HEAD_DIM
    kv_spec = pl.BlockSpec((None, s, HEAD_DIM), lambda bi, j, i: (layer, bi, j))
    in_specs = [pl.BlockSpec((tq, gw), lambda bi, j, i: (bi * nq + i, qcol // gw + j)), kv_spec, kv_spec]
    args = [q, k, v]
    scratch = [pltpu.VMEM((g * tq, HEAD_DIM), F32)]
    if mode == "fox":
        for h in range(g):
            in_specs.append(pl.BlockSpec((None, 1, s), lambda bi, j, i, h=h: (j * g + h, 0, bi)))
            args.append(frow)
        scratch += [pltpu.VMEM((g * tq, LANES), F32), pltpu.VMEM((g * tq, LANES), F32)]
    return pl.pallas_call(
        functools.partial(_prompt_attn_kernel, mode=mode, g=g, tq=tq, sub=sub, cs=HEAD_DIM ** -0.5 * LOG2E),
        grid=(b, kv, nq),
        in_specs=in_specs,
        out_specs=pl.BlockSpec((tq, gw), lambda bi, j, i: (bi * nq + i, j)),
        out_shape=jax.ShapeDtypeStruct((b * s, kv * gw), BF16),
        scratch_shapes=scratch,
        compiler_params=_params("parallel", "parallel", "parallel"),
        name="prompt_" + mode,
    )(*args)


def _even_sample_kernel(pt_ref, qa_ref, qb_ref, kan_ref, van_ref, kbn_ref, vbn_ref, lfn_ref,
                        ka_hbm, va_hbm, kb_hbm, vb_hbm, lf_hbm, oa_ref, ob_ref,
                        kbuf, lfbuf, sem, acca_ref, c_ref, accb_ref, m_ref, l_ref, rc_ref,
                        *, layer, npg, cp, nb, kv, g, tt, page, cs):
    b = pl.program_id(0)
    nc = npg // cp
    total = pl.num_programs(0) * nc
    nh = kv * g
    rows = nh * tt
    gr = g * tt
    prow = page * kv
    caches = (ka_hbm, va_hbm, kb_hbm, vb_hbm)
    m_inc = _tri(page, lower_eq=True)
    m_inc2 = _tri(page, lower_eq=True, stack=2)

    def chunk_copies(slot, phys):
        out = []
        for u in range(cp):
            pg = phys(u)
            for a, cache in enumerate(caches):
                out.append(pltpu.make_async_copy(cache.at[layer, pg], kbuf.at[slot, a, pl.ds(u * prow, prow)], sem.at[slot]))
            out.append(pltpu.make_async_copy(lf_hbm.at[layer, pg], lfbuf.at[slot, pl.ds(u * nh, nh)], sem.at[slot]))
        return out

    def start_chunk(gid):
        row = gid // nc
        base = row * npg + npg - (gid - row * nc + 1) * cp
        for c in chunk_copies(gid % nb, lambda u: pt_ref[base + u]):
            c.start()

    @pl.when(b == 0)
    def _():
        for n in range(nb - 1):
            start_chunk(jnp.int32(n))

    def stack_heads(q):
        return jnp.concatenate([q[:, h * HEAD_DIM:(h + 1) * HEAD_DIM] for h in range(nh)], axis=0).astype(BF16)

    def rep_rows(a):
        return jnp.concatenate([jnp.broadcast_to(a[h:h + 1, :], (tt, a.shape[1])) for h in range(nh)], axis=0)

    def scores(q, kblocks, c):
        return jnp.concatenate(
            [_dot_nt(q[k * gr:(k + 1) * gr], kblocks[k]) for k in range(kv)], axis=0) * c

    def weighted(w, vblocks):
        wb = w.astype(BF16)
        return jnp.concatenate(
            [_dot(wb[k * gr:(k + 1) * gr], vblocks[k]) for k in range(kv)], axis=0)

    qa = stack_heads(qa_ref[...])
    qb = stack_heads(qb_ref[...])

    zpad = jnp.zeros((page - tt, HEAD_DIM), F32)

    def new_blocks(ref):
        x = ref[...]
        return [jnp.concatenate([x[:, k * HEAD_DIM:(k + 1) * HEAD_DIM], zpad], axis=0).astype(BF16)
                for k in range(kv)]

    t_of = lax.broadcasted_iota(jnp.int32, (rows, page), 0) & (tt - 1)
    col = lax.broadcasted_iota(jnp.int32, (rows, page), 1)
    mask = col < t_of
    ny = scores(qa, new_blocks(kan_ref), -cs)
    inc = _dot_split2(jnp.where(mask, _log2_sigmoid(ny), 0.0), m_inc2)
    acca_ref[...] = weighted(jnp.where(mask, jnp.exp2(inc - ny), 0.0), new_blocks(van_ref))
    c_ref[...] = jnp.broadcast_to(inc[:, 0:1], c_ref.shape)
    fnew = _dot_split3(lfn_ref[...], _tri(page, lower_eq=False))
    yb = scores(qb, new_blocks(kbn_ref), cs) - rep_rows(fnew) * LOG2E
    yb = jnp.where(col <= t_of, yb, NEG_INF)
    mx = jnp.max(yb, axis=1, keepdims=True)
    p = jnp.exp2(yb - mx)
    m_ref[...] = jnp.broadcast_to(mx, m_ref.shape)
    l_ref[...] = jnp.broadcast_to(jnp.sum(p, axis=1, keepdims=True), l_ref.shape)
    accb_ref[...] = weighted(p, new_blocks(vbn_ref))
    rc_ref[...] = jnp.zeros_like(rc_ref)

    def chunk(cc, carry):
        gid = b * nc + cc
        nxt = gid + nb - 1

        @pl.when(nxt < total)
        def _():
            start_chunk(nxt)

        slot = gid % nb
        for c in chunk_copies(slot, lambda u: 0):
            c.wait()

        def blocks(a):
            view = kbuf.at[slot, a]
            return [view[pl.ds(k, cp * page, stride=kv), :].astype(BF16) for k in range(kv)]

        ny = scores(qa, blocks(0), -cs)
        lk = _log2_sigmoid(ny)
        inc = _dot_split2(jnp.concatenate([lk[:, u * page:(u + 1) * page] for u in range(cp)], axis=0), m_inc2)
        c = c_ref[...]
        ws = [None] * cp
        for u in reversed(range(cp)):
            inc_u = inc[u * rows:(u + 1) * rows]
            ws[u] = jnp.exp2(c + inc_u - ny[:, u * page:(u + 1) * page])
            c = c + jnp.broadcast_to(inc_u[:, 0:1], c.shape)
        c_ref[...] = c
        acca_ref[...] += weighted(jnp.concatenate(ws, axis=1), blocks(1))

        lf = lfbuf[slot]
        incf = _dot_split3(lf, m_inc)
        rc = rc_ref[...]
        bias = [None] * cp
        for u in reversed(range(cp)):
            incf_u = incf[u * nh:(u + 1) * nh]
            bias[u] = rep_rows(rc + incf_u - lf[u * nh:(u + 1) * nh])
            rc = rc + jnp.broadcast_to(incf_u[:, 0:1], rc.shape)
        rc_ref[...] = rc
        yb = scores(qb, blocks(2), cs) + jnp.concatenate(bias, axis=1) * LOG2E
        m_prev = m_ref[...]
        m_new = jnp.maximum(m_prev, jnp.max(yb, axis=1, keepdims=True))
        alpha = jnp.exp2(m_prev - m_new)
        p = jnp.exp2(yb - _lane_tile(m_new, cp))
        l_ref[...] = alpha * l_ref[...] + jnp.sum(p, axis=1, keepdims=True)
        accb_ref[...] = alpha * accb_ref[...] + weighted(p, blocks(3))
        m_ref[...] = m_new
        return carry

    lax.fori_loop(0, nc, chunk, 0)

    def unstack(a):
        return jnp.concatenate([a[h * tt:(h + 1) * tt] for h in range(nh)], axis=1)

    oa_ref[...] = unstack(acca_ref[...])
    ob_ref[...] = unstack(accb_ref[...] / l_ref[...])


def _even_sample(q, knew, lfn, caches, lf_cache, page_table, layer, kv, g, tt):
    db, npg = page_table.shape
    page = caches[0].shape[2] // kv
    nh = kv * g
    cp = PAGES_PER_CHUNK if npg % PAGES_PER_CHUNK == 0 else 1
    nb = CHUNK_SLOTS
    assert db * (npg // cp) >= nb - 1
    qw = nh * HEAD_DIM
    kw = kv * HEAD_DIM
    new_spec = pl.BlockSpec((None, tt, kw), lambda b, pt: (layer, b, 0))
    in_specs = [pl.BlockSpec((tt, qw), lambda b, pt: (b, 0)), pl.BlockSpec((tt, qw), lambda b, pt: (b, 1)),
                new_spec, new_spec, new_spec, new_spec,
                pl.BlockSpec((None, nh, LANES), lambda b, pt: (b, 0, 0))]
    in_specs += [pl.BlockSpec(memory_space=pl.ANY)] * 5
    rows = nh * tt
    return pl.pallas_call(
        functools.partial(_even_sample_kernel, layer=layer, npg=npg, cp=cp, nb=nb, kv=kv, g=g, tt=tt,
                          page=page, cs=HEAD_DIM ** -0.5 * LOG2E),
        grid_spec=pltpu.PrefetchScalarGridSpec(
            num_scalar_prefetch=1,
            grid=(db,),
            in_specs=in_specs,
            out_specs=[pl.BlockSpec((tt, qw), lambda b, pt: (b, 0)), pl.BlockSpec((tt, qw), lambda b, pt: (b, 0))],
            scratch_shapes=[pltpu.VMEM((nb, 4, cp * page * kv, HEAD_DIM), F32),
                            pltpu.VMEM((nb, cp * nh, page), F32),
                            pltpu.SemaphoreType.DMA((nb,)),
                            pltpu.VMEM((rows, HEAD_DIM), F32), pltpu.VMEM((rows, LANES), F32),
                            pltpu.VMEM((rows, HEAD_DIM), F32), pltpu.VMEM((rows, LANES), F32),
                            pltpu.VMEM((rows, LANES), F32), pltpu.VMEM((nh, LANES), F32)],
        ),
        out_shape=[jax.ShapeDtypeStruct((db * tt, qw), F32), jax.ShapeDtypeStruct((db * tt, qw), F32)],
        compiler_params=_params("arbitrary"),
        name="even_sample",
    )(page_table.reshape(-1), q, q, *knew, lfn, *caches, lf_cache)


def _out_proj_kernel(a1_ref, a2_ref, w1_ref, w2_ref, x_ref, o_ref):
    o_ref[...] = (x_ref[...] + _dot(a1_ref[...].astype(BF16), w1_ref[...])
                  + _dot(a2_ref[...].astype(BF16), w2_ref[...]))


def _out_proj(a1, a2, col2, w, x, row0):
    n = a1.shape[0]
    d = x.shape[1]
    kh = w.shape[0] // 2
    tm = _pick_tile(n, 512, LANES)
    assert row0 % tm == 0
    r0 = row0 // tm
    return pl.pallas_call(
        _out_proj_kernel,
        grid=(n // tm,),
        in_specs=[
            pl.BlockSpec((tm, kh), lambda i: (i, 0)),
            pl.BlockSpec((tm, kh), lambda i: (i, col2)),
            pl.BlockSpec((kh, d), lambda i: (0, 0)),
            pl.BlockSpec((kh, d), lambda i: (1, 0)),
            pl.BlockSpec((tm, d), lambda i: (r0 + i, 0)),
        ],
        out_specs=pl.BlockSpec((tm, d), lambda i: (r0 + i, 0)),
        out_shape=jax.ShapeDtypeStruct(x.shape, F32),
        input_output_aliases={4: 0},
        compiler_params=_params("parallel"),
        name="out_proj",
    )(a1, a2, w, w, x)


def _rope_table_kernel(pos_ref, inv_ref, cos_ref, sin_ref):
    ang = pos_ref[...] * inv_ref[...]
    lane = lax.broadcasted_iota(jnp.int32, ang.shape, 1)
    first = (lane & (HD_C - 1)) < HD_C // 2
    cos_ref[...] = jnp.cos(ang)
    sin_ref[...] = jnp.where(first, -1.0, 1.0) * jnp.sin(ang)


def _rope_table(pos, inv):
    t = pos.shape[0]
    tm = _pick_tile(t, 768, LANES)
    return pl.pallas_call(
        _rope_table_kernel,
        grid=(t // tm,),
        in_specs=[pl.BlockSpec((tm, LANES), lambda i: (i, 0)), pl.BlockSpec((1, LANES), lambda i: (0, 0))],
        out_specs=[pl.BlockSpec((tm, LANES), lambda i: (i, 0))] * 2,
        out_shape=[jax.ShapeDtypeStruct((t, LANES), F32)] * 2,
        compiler_params=_params("parallel"),
        name="rope_table",
    )(pos, inv)


def _proj_odd_kernel(x_ref, g_ref, w_ref, cos_ref, sin_ref, q_ref, kv_ref, h_ref, *, nq, rope_cols):
    j = pl.program_id(1)

    @pl.when(j == 0)
    def _():
        h_ref[...] = _rms(x_ref[...], g_ref[...]).astype(BF16)

    y = _dot(h_ref[...], w_ref[...].astype(BF16))
    tn = y.shape[1]
    half = HD_C // 2
    lane = lax.broadcasted_iota(jnp.int32, y.shape, 1)
    first = (lane & (HD_C - 1)) < half
    partner = jnp.where(first, pltpu.roll(y, tn - half, 1), pltpu.roll(y, half, 1))
    reps = tn // LANES
    roped = y * _lane_tile(cos_ref[...], reps) + partner * _lane_tile(sin_ref[...], reps)
    out = jnp.where(lane + j * tn < rope_cols, roped, y)

    @pl.when(j < nq)
    def _():
        q_ref[...] = out

    @pl.when(j == nq)
    def _():
        kv_ref[...] = out


def _proj_odd(x, row0, n, g, w_all, layer, cos_t, sin_t, qw, kwc):
    d = x.shape[1]
    tn = 2 * kwc
    assert qw % tn == 0 and w_all.shape[2] == qw + tn
    nq = qw // tn
    tm = _pick_tile(n, 1024, LANES)
    assert row0 % tm == 0
    r0 = row0 // tm
    return pl.pallas_call(
        functools.partial(_proj_odd_kernel, nq=nq, rope_cols=qw + kwc),
        grid=(n // tm, nq + 1),
        in_specs=[
            pl.BlockSpec((tm, d), lambda i, j: (r0 + i, 0)),
            pl.BlockSpec((1, d), lambda i, j: (0, 0)),
            pl.BlockSpec((None, d, tn), lambda i, j: (layer, 0, j)),
            pl.BlockSpec((tm, LANES), lambda i, j: (r0 + i, 0)),
            pl.BlockSpec((tm, LANES), lambda i, j: (r0 + i, 0)),
        ],
        out_specs=[pl.BlockSpec((tm, tn), lambda i, j: (i, jnp.minimum(j, nq - 1))),
                   pl.BlockSpec((tm, tn), lambda i, j: (i, 0))],
        out_shape=[jax.ShapeDtypeStruct((n, qw), F32), jax.ShapeDtypeStruct((n, tn), F32)],
        scratch_shapes=[pltpu.VMEM((tm, d), BF16)],
        compiler_params=_params("parallel", "arbitrary"),
        name="proj_odd",
    )(x, g.reshape(1, d), w_all, cos_t, sin_t)


def _swa_group(q, kband, vband, mask, sink_col, g, cs):
    tq = q.shape[0]
    q8 = jnp.concatenate([q[:, h * HD_C:(h + 1) * HD_C] for h in range(g)], axis=0).astype(BF16)
    y = _dot_nt(q8, kband.astype(BF16)) * cs
    y = jnp.where(mask, y, NEG_INF)
    m = jnp.maximum(jnp.max(y, axis=1, keepdims=True), sink_col)
    p = jnp.exp2(y - m)
    denom = jnp.sum(p, axis=1, keepdims=True) + jnp.exp2(sink_col - m)
    o = _dot(p.astype(BF16), vband.astype(BF16)) / denom
    return jnp.concatenate([o[h * tq:(h + 1) * tq] for h in range(g)], axis=1)


def _sink_col(sinks_ref, k, g, tq):
    return jnp.concatenate([jnp.full((tq, 1), sinks_ref[k * g + h] * LOG2E, F32) for h in range(g)], axis=0)


def _pair_diag(ref_p, ref_c, col, odd):
    x = jnp.concatenate([ref_p[:, col * LANES:(col + 1) * LANES], ref_c[:, col * LANES:(col + 1) * LANES]], axis=0)
    low = lax.broadcasted_iota(jnp.int32, x.shape, 1) < HD_C
    xr = pltpu.roll(x, HD_C, 1)
    top = jnp.where(low, xr if odd else x, 0.0)
    bot = jnp.where(low, 0.0, x if odd else xr)
    return jnp.concatenate([top, bot], axis=0)


def _swa_prompt_kernel(sinks_ref, q_ref, kvc_ref, kvp_ref, o_ref, *, kv, g, cs):
    i = pl.program_id(1)
    tq = q_ref.shape[0]
    nk = 2 * tq
    npair = g // 2
    cols = npair * tq
    kcols = kv * HD_C // LANES
    kj = lax.broadcasted_iota(jnp.int32, (2 * nk, cols), 0) & (nk - 1)
    qi = lax.broadcasted_iota(jnp.int32, (2 * nk, cols), 1) & (tq - 1)
    bias = jnp.where((kj > qi) & (kj <= qi + tq) & ((kj >= tq) | (i > 0)), 0.0, NEG_INF)
    first = lax.broadcasted_iota(jnp.int32, (LANES, cols), 0) < HD_C
    for k in range(kv):
        kd = _pair_diag(kvp_ref, kvc_ref, k // 2, k % 2).astype(BF16)
        vdt = _pair_diag(kvp_ref, kvc_ref, kcols + k // 2, k % 2).T.astype(BF16)
        q4 = jnp.concatenate([q_ref[:, (k * npair + j) * LANES:(k * npair + j + 1) * LANES] for j in range(npair)],
                             axis=0).astype(BF16)
        y = _dot_nt(kd, q4) * cs + bias
        ps, dens = [], []
        for h in range(2):
            yh = y[h * nk:(h + 1) * nk]
            sink = jnp.concatenate([jnp.full((1, tq), sinks_ref[k * g + 2 * j + h] * LOG2E, F32) for j in range(npair)], axis=1)
            m = jnp.maximum(jnp.max(yh, axis=0, keepdims=True), sink)
            p = jnp.exp2(yh - m)
            ps.append(p.astype(BF16))
            dens.append(jnp.sum(p, axis=0, keepdims=True) + jnp.exp2(sink - m))
        ot = _dot(vdt, jnp.concatenate(ps, axis=0)) / jnp.where(first, dens[0], dens[1])
        o = ot.T
        for j in range(npair):
            c0 = (k * npair + j) * LANES
            o_ref[:, c0:c0 + LANES] = o[j * tq:(j + 1) * tq].astype(o_ref.dtype)


def _swa_prompt(q, kvn, sinks, b, s, kv, g):
    nq = s // WINDOW
    qw = kv * g * HD_C
    kw2 = 2 * kv * HD_C
    return pl.pallas_call(
        functools.partial(_swa_prompt_kernel, kv=kv, g=g, cs=HD_C ** -0.5 * LOG2E),
        grid_spec=pltpu.PrefetchScalarGridSpec(
            num_scalar_prefetch=1,
            grid=(b, nq),
            in_specs=[pl.BlockSpec((WINDOW, qw), lambda bi, i, sk: (bi * nq + i, 0)),
                      pl.BlockSpec((WINDOW, kw2), lambda bi, i, sk: (bi * nq + i, 0)),
                      pl.BlockSpec((WINDOW, kw2), lambda bi, i, sk: (jnp.maximum(bi * nq + i - 1, 0), 0))],
            out_specs=pl.BlockSpec((WINDOW, qw), lambda bi, i, sk: (bi * nq + i, 0)),
        ),
        out_shape=jax.ShapeDtypeStruct((b * s, qw), BF16),
        compiler_params=_params("parallel", "parallel"),
        name="swa_prompt",
    )(sinks, q, kvn, kvn)


def _swa_sample_kernel(sinks_ref, q_ref, kvn_ref, kbuf_ref, vbuf_ref, o_ref, *, kv, g, tt, cs):
    gw = g * HD_C
    kw = kv * HD_C
    nk = 2 * WINDOW
    t_of = lax.broadcasted_iota(jnp.int32, (g * tt, nk), 0) & (tt - 1)
    col = lax.broadcasted_iota(jnp.int32, (g * tt, nk), 1)
    mask = ((col < WINDOW) & (col > t_of)) | ((col >= WINDOW) & (col - WINDOW <= t_of))
    zpad = jnp.zeros((WINDOW - tt, HD_C), F32)
    for k in range(kv):
        ks = slice(k * HD_C, (k + 1) * HD_C)
        vs = slice(kw + k * HD_C, kw + (k + 1) * HD_C)
        kband = jnp.concatenate([kbuf_ref[:, ks], kvn_ref[:, ks], zpad], axis=0)
        vband = jnp.concatenate([vbuf_ref[:, ks], kvn_ref[:, vs], zpad], axis=0)
        o = _swa_group(q_ref[:, k * gw:(k + 1) * gw], kband, vband, mask, _sink_col(sinks_ref, k, g, tt), g, cs)
        o_ref[:, k * gw:(k + 1) * gw] = o


def _swa_sample(q, kvn, sinks, kbuf, vbuf, layer, kv, g, tt):
    db = kbuf.shape[1]
    qw = kv * g * HD_C
    kw = kv * HD_C
    buf_spec = pl.BlockSpec((None, None, WINDOW, kw), lambda b, sk: (layer, b, 0, 0))
    return pl.pallas_call(
        functools.partial(_swa_sample_kernel, kv=kv, g=g, tt=tt, cs=HD_C ** -0.5 * LOG2E),
        grid_spec=pltpu.PrefetchScalarGridSpec(
            num_scalar_prefetch=1,
            grid=(db,),
            in_specs=[pl.BlockSpec((tt, qw), lambda b, sk: (b, 0)),
                      pl.BlockSpec((tt, 2 * kw), lambda b, sk: (b, 0)),
                      buf_spec, buf_spec],
            out_specs=pl.BlockSpec((tt, qw), lambda b, sk: (b, 0)),
        ),
        out_shape=jax.ShapeDtypeStruct((db * tt, qw), F32),
        compiler_params=_params("parallel"),
        name="swa_sample",
    )(sinks, q, kvn, kbuf, vbuf)


def _final_norm_kernel(x_ref, g_ref, o_ref):
    o_ref[...] = _rms(x_ref[...], g_ref[...])


def _final_norm(x, row0, n, g):
    d = x.shape[1]
    tm = _pick_tile(n, 1024, LANES)
    assert row0 % tm == 0
    r0 = row0 // tm
    return pl.pallas_call(
        _final_norm_kernel,
        grid=(n // tm,),
        in_specs=[pl.BlockSpec((tm, d), lambda i: (r0 + i, 0)), pl.BlockSpec((1, d), lambda i: (0, 0))],
        out_specs=pl.BlockSpec((tm, d), lambda i: (i, 0)),
        out_shape=jax.ShapeDtypeStruct((n, d), F32),
        compiler_params=_params("parallel"),
        name="final_norm",
    )(x, g.reshape(1, d))


def kernel(x_prompt, x_sample, cache_sb_k, cache_sb_v, cache_fox_k, cache_fox_v, cache_fox_logf, cache_swa_k, cache_swa_v, page_table, norm_ffn1, norm_mix, norm_ffn2, norm_final, w_ffn_in, w_ffn_out, w_in_even, b_forget, w_out_even, w_in_odd, sinks, w_out_odd):
    b, s, d = x_prompt.shape
    db, tt, _ = x_sample.shape
    depth = norm_mix.shape[0]
    n_even, n_phys, page, kv_a, _ = cache_sb_k.shape
    n_odd = cache_swa_k.shape[0]
    kv_c = cache_swa_k.shape[3]
    h_b = cache_fox_logf.shape[3]
    kv_b = cache_fox_k.shape[3]
    h_a = (w_in_even.shape[2] - h_b - 2 * (kv_a + kv_b) * HEAD_DIM) // HEAD_DIM - h_b
    g_a, g_b = h_a // kv_a, h_b // kv_b
    h_c = w_out_odd.shape[1] // HD_C
    g_c = h_c // kv_c
    assert kv_a == kv_b and g_a == g_b and h_b == SUBLANES and tt == SUBLANES
    bs = b * s
    ns = db * tt
    past_len = page_table.shape[1] * page
    kw = kv_a * HEAD_DIM
    qwa = h_a * HEAD_DIM
    qw = qwa + h_b * HEAD_DIM

    x = jnp.concatenate([x_prompt.reshape(bs, d), x_sample.reshape(ns, d)], axis=0)
    w_in_b = w_ffn_in[0, 0].astype(BF16)
    w_out_b = w_ffn_out[0, 0].astype(BF16)

    paged = [c.reshape(n_even, n_phys, page * kv_a, HEAD_DIM) for c in (cache_sb_k, cache_sb_v, cache_fox_k, cache_fox_v)]
    lf_cache = jnp.swapaxes(cache_fox_logf, 2, 3)
    swa_kbuf = cache_swa_k.reshape(n_odd, db, WINDOW, kv_c * HD_C)
    swa_vbuf = cache_swa_v.reshape(n_odd, db, WINDOW, kv_c * HD_C)

    half = HD_C // 2
    inv_freq = ROPE_THETA ** (-jnp.arange(half, dtype=F32) / half)
    inv = jnp.tile(inv_freq, LANES // half).reshape(1, LANES)
    pos = jnp.concatenate([jnp.tile(jnp.arange(s, dtype=jnp.int32), b),
                           jnp.tile(past_len + jnp.arange(tt, dtype=jnp.int32), db)]).astype(F32)
    cos_t, sin_t = _rope_table(jnp.broadcast_to(pos[:, None], (bs + ns, LANES)), inv)

    kv_p = [jnp.zeros((n_even, bs, kw), F32) for _ in range(4)]
    kv_s = [jnp.zeros((n_even, ns, kw), F32) for _ in range(4)]
    lf_p, lf_s = [], []
    odd_p = [[], []]
    odd_s = [[], []]

    for l in range(depth):
        x, w_in_b, w_out_b = _ffn(x, norm_ffn1[l], w_in_b, w_out_b, (w_ffn_in, w_ffn_out, l, 1))
        i = l // 2
        if l % 2 == 0:
            w_f = jnp.pad(w_in_even[i, :, qw + 4 * kw:], ((0, 0), (0, LANES - h_b))).astype(BF16)
            b_f = jnp.pad(b_forget[i], (0, LANES - h_b)).reshape(1, LANES)
            q_p, kv_p, lfp, lft_p = _proj_even(x, 0, bs, norm_mix[l], w_in_even, w_f, b_f, qwa, qw, kw, i, n_even, kv_p)
            q_s, kv_s, lfs, lft_s = _proj_even(x, bs, ns, norm_mix[l], w_in_even, w_f, b_f, qwa, qw, kw, i, n_even, kv_s)
            lf_p.append(lfp[:, :h_b].reshape(b, s, h_b))
            lf_s.append(lfs[:, :h_b].reshape(db, tt, h_b))
            frow = _fcum(lft_p, b, s).reshape(h_b, 1, bs)
            oa_p = _prompt_attn(q_p, kv_p[0], kv_p[1], None, "sb", i, b, s, kv_a, g_a, 0)
            ob_p = _prompt_attn(q_p, kv_p[2], kv_p[3], frow, "fox", i, b, s, kv_b, g_b, qwa)
            lfn = jnp.pad(lft_s.reshape(h_b, db, tt).transpose(1, 0, 2), ((0, 0), (0, 0), (0, LANES - tt)))
            oa_s, ob_s = _even_sample(q_s, kv_s, lfn, paged, lf_cache, page_table, i, kv_a, g_a, tt)
            w_out = w_out_even[i].astype(BF16)
            x = _out_proj(oa_p, ob_p, 0, w_out, x, 0)
            x = _out_proj(oa_s, ob_s, 0, w_out, x, bs)
        else:
            qwc = h_c * HD_C
            kwc = kv_c * HD_C
            q_p, kvn_p = _proj_odd(x, 0, bs, norm_mix[l], w_in_odd, i, cos_t, sin_t, qwc, kwc)
            q_s, kvn_s = _proj_odd(x, bs, ns, norm_mix[l], w_in_odd, i, cos_t, sin_t, qwc, kwc)
            o_p = _swa_prompt(q_p, kvn_p, sinks[i], b, s, kv_c, g_c)
            o_s = _swa_sample(q_s, kvn_s, sinks[i], swa_kbuf, swa_vbuf, i, kv_c, g_c, tt)
            w_out = w_out_odd[i].astype(BF16)
            x = _out_proj(o_p, o_p, 1, w_out, x, 0)
            x = _out_proj(o_s, o_s, 1, w_out, x, bs)
            for n, buf in enumerate((cache_swa_k[i], cache_swa_v[i])):
                cols = slice(n * kwc, (n + 1) * kwc)
                odd_p[n].append(kvn_p.reshape(b, s, 2 * kwc)[:, s - WINDOW:, cols].reshape(b, WINDOW, kv_c, HD_C))
                odd_s[n].append(jnp.concatenate([buf[:, tt:], kvn_s[:, cols].reshape(db, tt, kv_c, HD_C)], axis=1))
        x, w_in_b, w_out_b = _ffn(x, norm_ffn2[l], w_in_b, w_out_b,
                                  (w_ffn_in, w_ffn_out, l + 1, 0) if l + 1 < depth else None)

    outs = [_final_norm(x, 0, bs, norm_final).reshape(b, s, d), _final_norm(x, bs, ns, norm_final).reshape(db, tt, d)]
    outs += [a.reshape(n_even, b, s, kv_a, HEAD_DIM) for a in kv_p] + [jnp.stack(lf_p)]
    outs += [jnp.stack(a) for a in odd_p]
    outs += [a.reshape(n_even, db, tt, kv_a, HEAD_DIM) for a in kv_s] + [jnp.stack(lf_s)]
    outs += [jnp.stack(a) for a in odd_s]
    return tuple(outs)
```

```python
import functools
import math

import jax
import jax.numpy as jnp
from jax import lax
from jax.experimental import pallas as pl
from jax.experimental.pallas import tpu as pltpu

F32 = jnp.float32
BF16 = jnp.bfloat16

HEAD_DIM = 128
HD_C = 64
WINDOW = 128
ROPE_THETA = 10000.0
EPS = 1e-6
NEG_INF = -1e30
LOG2E = math.log2(math.e)
LANES = 128
SUBLANES = 8
VMEM_LIMIT = 56 * 1024 * 1024
PAGES_PER_CHUNK = 8
CHUNK_SLOTS = 3


def _pick_tile(n, target, mult):
    best = None
    for t in range(mult, min(n, target) + 1, mult):
        if n % t == 0:
            best = t
    assert best is not None, (n, target, mult)
    return best


def _params(*sem):
    return pltpu.CompilerParams(dimension_semantics=sem, vmem_limit_bytes=VMEM_LIMIT)


def _dot(a, b):
    return jnp.dot(a, b, preferred_element_type=F32)


def _dot_nt(a, b):
    return lax.dot_general(a, b, (((1,), (1,)), ((), ())), preferred_element_type=F32)


def _rms(x, g):
    ms = jnp.mean(x * x, axis=-1, keepdims=True)
    return x * lax.rsqrt(ms + EPS) * g


def _log_sigmoid(x):
    return jnp.minimum(x, 0.0) - jnp.log1p(jnp.exp(-jnp.abs(x)))


def _log2_sigmoid(ny):
    neg_abs = lax.bitcast_convert_type(lax.bitcast_convert_type(ny, jnp.uint32) | jnp.uint32(0x80000000), F32)
    return jnp.minimum(ny, 0.0) - jnp.log2(1.0 + jnp.exp2(neg_abs))


def _split2(a):
    hi = a.astype(BF16)
    lo = (a - hi.astype(F32)).astype(BF16)
    return hi, lo


def _dot_split2(a, m01x2):
    return _dot(jnp.concatenate(_split2(a), axis=1), m01x2)


def _dot_split3(a, m01):
    a1 = a.astype(BF16)
    r1 = a - a1.astype(F32)
    a2 = r1.astype(BF16)
    a3 = (r1 - a2.astype(F32)).astype(BF16)
    return _dot(a1, m01) + _dot(a2, m01) + _dot(a3, m01)


def _tri(n, lower_eq, stack=1):
    r = lax.broadcasted_iota(jnp.int32, (stack * n, n), 0) & (n - 1)
    c = lax.broadcasted_iota(jnp.int32, (stack * n, n), 1)
    m = (r >= c) if lower_eq else (r <= c)
    return jnp.where(m, 1.0, 0.0).astype(BF16)


def _lane_tile(a, reps):
    return a if reps == 1 else jnp.concatenate([a] * reps, axis=1)


def _ffn_kernel(x_ref, g_ref, wg_ref, wu_ref, wo_ref, *rest, convert_next):
    if convert_next:
        nin_ref, nout_ref, o_ref, cin_ref, cout_ref, h_ref = rest
        cin_ref[...] = nin_ref[...].astype(BF16)
        cout_ref[...] = nout_ref[...].astype(BF16)
    else:
        o_ref, h_ref = rest

    @pl.when(pl.program_id(1) == 0)
    def _():
        x = x_ref[...]
        h_ref[...] = _rms(x, g_ref[...]).astype(BF16)
        o_ref[...] = x

    h = h_ref[...]
    gate = _dot(h, wg_ref[...])
    up = _dot(h, wu_ref[...])
    a = (gate * jax.nn.sigmoid(gate)) * (0.5 * up)
    o_ref[...] += _dot(a.astype(BF16), wo_ref[...])


def _pow2_blocks(n, limit, mult):
    k = 1
    while 2 * k <= limit and n % (2 * k) == 0 and (n // (2 * k)) % mult == 0:
        k *= 2
    return k


def _ffn(x, g, w_in, w_out, nxt=None):
    t, d = x.shape
    f = w_out.shape[0]
    tm = _pick_tile(t, 768, LANES)
    tf = _pick_tile(f, 512, LANES)
    nf = f // tf
    ni = t // tm
    in_specs = [
        pl.BlockSpec((tm, d), lambda i, j: (i, 0)),
        pl.BlockSpec((1, d), lambda i, j: (0, 0)),
        pl.BlockSpec((d, tf), lambda i, j: (0, j)),
        pl.BlockSpec((d, tf), lambda i, j: (0, nf + j)),
        pl.BlockSpec((tf, d), lambda i, j: (j, 0)),
    ]
    args = [x, g.reshape(1, d), w_in, w_in, w_out]
    out_specs = [pl.BlockSpec((tm, d), lambda i, j: (i, 0))]
    out_shape = [jax.ShapeDtypeStruct((t, d), F32)]
    if nxt is not None:
        w_in_all, w_out_all, ln, hn = nxt
        nr = _pow2_blocks(d, ni, 2 * SUBLANES)
        ncol = _pow2_blocks(d, ni, LANES)
        rb, cb = d // nr, 2 * f // nf
        rb2, cb2 = f // nf, d // ncol

        def in_idx(i, j):
            return jnp.minimum(i, nr - 1), jnp.where(i < nr, j, nf - 1)

        def out_idx(i, j):
            return jnp.where(i < ncol, j, nf - 1), jnp.minimum(i, ncol - 1)

        in_specs += [pl.BlockSpec((None, None, rb, cb), lambda i, j: (ln, hn, *in_idx(i, j))),
                     pl.BlockSpec((None, None, rb2, cb2), lambda i, j: (ln, hn, *out_idx(i, j)))]
        args += [w_in_all, w_out_all]
        out_specs += [pl.BlockSpec((rb, cb), in_idx), pl.BlockSpec((rb2, cb2), out_idx)]
        out_shape += [jax.ShapeDtypeStruct((d, 2 * f), BF16), jax.ShapeDtypeStruct((f, d), BF16)]
    outs = pl.pallas_call(
        functools.partial(_ffn_kernel, convert_next=nxt is not None),
        grid=(ni, nf),
        in_specs=in_specs,
        out_specs=out_specs,
        out_shape=out_shape,
        scratch_shapes=[pltpu.VMEM((tm, d), BF16)],
        compiler_params=_params("arbitrary", "arbitrary"),
        name="ffn",
    )(*args)
    return outs if nxt is not None else (outs[0], None, None)


def _proj_even_kernel(*refs, nq, nkv):
    x_ref, g_ref, w_ref, wf_ref, bf_ref = refs[:5]
    outs = refs[5 + nkv:]
    q_ref, kv_refs, lf_ref, lft_ref, h_ref = outs[0], outs[1:1 + nkv], outs[1 + nkv], outs[2 + nkv], outs[3 + nkv]
    j = pl.program_id(1)

    @pl.when(j == 0)
    def _():
        hb = _rms(x_ref[...], g_ref[...]).astype(BF16)
        h_ref[...] = hb
        lf = _log_sigmoid(_dot_nt(hb, wf_ref[...].astype(BF16)) + bf_ref[...])
        lf_ref[...] = lf
        lft_ref[...] = lf.T[:SUBLANES, :]

    y = _dot_nt(h_ref[...], w_ref[...].astype(BF16))

    @pl.when(j < nq)
    def _():
        q_ref[...] = y

    for n, ref in enumerate(kv_refs):
        @pl.when(j == nq + n)
        def _(ref=ref):
            ref[...] = y


def _proj_even(x, row0, n, g, w_all, w_f, b_f, qwa, qw, kw, layer, n_layers, prev):
    d = x.shape[1]
    tm = _pick_tile(n, 1024, LANES)
    assert row0 % tm == 0 and qw % kw == 0 and qwa % kw == 0
    r0 = row0 // tm
    nq = qw // kw
    nqa = qwa // kw
    nkv = len(prev)

    def wcol(j):
        return jnp.where(j < nqa, j, jnp.where(j < nq, j + 2, jnp.where(j < nq + 2, j - (nq - nqa), j)))

    in_specs = [
        pl.BlockSpec((tm, d), lambda i, j: (r0 + i, 0)),
        pl.BlockSpec((1, d), lambda i, j: (0, 0)),
        pl.BlockSpec((None, kw, d), lambda i, j: (layer, wcol(j), 0)),
        pl.BlockSpec((LANES, d), lambda i, j: (0, 0)),
        pl.BlockSpec((1, LANES), lambda i, j: (0, 0)),
    ]
    args = [x, g.reshape(1, d), w_all, w_f, b_f]
    aliases = {}
    for m, arr in enumerate(prev):
        in_specs.append(pl.BlockSpec(memory_space=pl.ANY))
        args.append(arr)
        aliases[5 + m] = 1 + m
    out_specs = [pl.BlockSpec((tm, kw), lambda i, j: (i, jnp.minimum(j, nq - 1)))]
    out_shape = [jax.ShapeDtypeStruct((n, qw), F32)]
    for _ in range(nkv):
        out_specs.append(pl.BlockSpec((None, tm, kw), lambda i, j: (layer, i, 0)))
        out_shape.append(jax.ShapeDtypeStruct((n_layers, n, kw), F32))
    out_specs += [pl.BlockSpec((tm, LANES), lambda i, j: (i, 0)), pl.BlockSpec((SUBLANES, tm), lambda i, j: (0, i))]
    out_shape += [jax.ShapeDtypeStruct((n, LANES), F32), jax.ShapeDtypeStruct((SUBLANES, n), F32)]
    outs = pl.pallas_call(
        functools.partial(_proj_even_kernel, nq=nq, nkv=nkv),
        grid=(n // tm, nq + nkv),
        in_specs=in_specs,
        out_specs=out_specs,
        out_shape=out_shape,
        scratch_shapes=[pltpu.VMEM((tm, d), BF16)],
        input_output_aliases=aliases,
        compiler_params=_params("parallel", "arbitrary"),
        name="proj_even",
    )(*args)
    return outs[0], list(outs[1:1 + nkv]), outs[1 + nkv], outs[2 + nkv]


def _fcum_kernel(lft_ref, o_ref, carry_ref):
    @pl.when(pl.program_id(1) == 0)
    def _():
        carry_ref[...] = jnp.zeros_like(carry_ref)

    blk = lft_ref.shape[1]
    cum = _dot_split3(lft_ref[...], _tri(blk, lower_eq=False)) + carry_ref[:, 0:1]
    o_ref[...] = cum
    carry_ref[...] = jnp.broadcast_to(cum[:, blk - 1:blk], carry_ref.shape)


def _fcum(lft, b, s):
    blk = _pick_tile(s, 512, LANES)
    nb = s // blk
    return pl.pallas_call(
        _fcum_kernel,
        grid=(b, nb),
        in_specs=[pl.BlockSpec((SUBLANES, blk), lambda i, j: (0, i * nb + j))],
        out_specs=pl.BlockSpec((SUBLANES, blk), lambda i, j: (0, i * nb + j)),
        out_shape=jax.ShapeDtypeStruct((SUBLANES, b * s), F32),
        scratch_shapes=[pltpu.VMEM((SUBLANES, LANES), F32)],
        compiler_params=_params("parallel", "arbitrary"),
        name="fcum",
    )(lft)


def _block_scores(q2, k_ref, start, sub, c):
    if not isinstance(start, int):
        start = pl.multiple_of(start, sub)
    return _dot_nt(q2, k_ref[pl.ds(start, sub), :].astype(BF16)) * c


def _sb_block(ny, v_ref, start, sub, mask, m_inc2, c, acc_ref):
    vs = v_ref[pl.ds(pl.multiple_of(start, sub), sub), :].astype(BF16)
    lk = _log2_sigmoid(ny)
    if mask is not None:
        lk = jnp.where(mask, lk, 0.0)
    inc = _dot_split2(lk, m_inc2)
    w = jnp.exp2(_lane_tile(c, sub // LANES) + inc - ny)
    if mask is not None:
        w = jnp.where(mask, w, 0.0)
    acc_ref[...] += _dot(w.astype(BF16), vs)
    return c + jnp.broadcast_to(inc[:, 0:1], c.shape)


def _fox_block(y, v_ref, fk, start, sub, mask, m_ref, l_ref, acc_ref):
    vs = v_ref[pl.ds(pl.multiple_of(start, sub), sub), :].astype(BF16)
    y = y - fk
    if mask is not None:
        y = jnp.where(mask, y, NEG_INF)
    m_prev = m_ref[...]
    m_new = jnp.maximum(m_prev, jnp.max(y, axis=1, keepdims=True))
    alpha = jnp.exp2(m_prev - m_new)
    p = jnp.exp2(y - _lane_tile(m_new, sub // LANES))
    l_ref[...] = alpha * l_ref[...] + jnp.sum(p, axis=1, keepdims=True)
    acc_ref[...] = alpha * acc_ref[...] + _dot(p.astype(BF16), vs)
    m_ref[...] = m_new


def _prompt_attn_kernel(*refs, mode, g, tq, sub, cs):
    if mode == "sb":
        q_ref, k_ref, v_ref, o_ref, acc_ref = refs
    else:
        q_ref, k_ref, v_ref, *fk_refs, o_ref, acc_ref, m_ref, l_ref = refs
    i = pl.program_id(2)
    nd = tq // sub
    rows = g * tq
    q = q_ref[...]
    q2 = jnp.concatenate([q[:, h * HEAD_DIM:(h + 1) * HEAD_DIM] for h in range(g)], axis=0).astype(BF16)
    qi = lax.broadcasted_iota(jnp.int32, (rows, sub), 0) & (tq - 1)
    kj = lax.broadcasted_iota(jnp.int32, (rows, sub), 1)
    acc_ref[...] = jnp.zeros_like(acc_ref)

    n_loop = i * nd
    if mode == "sb":
        m_inc = _tri(sub, lower_eq=True, stack=2)
        c = jnp.zeros((rows, LANES), F32)
        ny = _block_scores(q2, k_ref, i * tq + (nd - 1) * sub, sub, -cs)
        for d in reversed(range(nd)):
            start = i * tq + d * sub
            ny_next = _block_scores(q2, k_ref, jnp.maximum(start - sub, 0), sub, -cs)
            c = _sb_block(ny, v_ref, start, sub, kj + d * sub < qi, m_inc, c, acc_ref)
            ny = ny_next

        def body(jb, carry):
            c, ny = carry
            for d in reversed(range(nd)):
                start = (i - 1 - jb) * tq + d * sub
                ny_next = _block_scores(q2, k_ref, jnp.maximum(start - sub, 0), sub, -cs)
                c = _sb_block(ny, v_ref, start, sub, None, m_inc, c, acc_ref)
                ny = ny_next
            return c, ny

        lax.fori_loop(0, i, body, (c, ny))
        out = acc_ref[...]
    else:
        m_ref[...] = jnp.full_like(m_ref, NEG_INF)
        l_ref[...] = jnp.zeros_like(l_ref)

        def fk_rows(start):
            return jnp.concatenate(
                [jnp.broadcast_to(r[:, pl.ds(pl.multiple_of(start, sub), sub)] * LOG2E, (tq, sub)) for r in fk_refs],
                axis=0)

        for d in range(nd):
            start = i * tq + d * sub
            _fox_block(_block_scores(q2, k_ref, start, sub, cs), v_ref, fk_rows(start), start, sub,
                       kj + d * sub <= qi, m_ref, l_ref, acc_ref)

        def body(jj, carry):
            start = jj * sub
            _fox_block(_block_scores(q2, k_ref, start, sub, cs), v_ref, fk_rows(start), start, sub,
                       None, m_ref, l_ref, acc_ref)
            return carry

        lax.fori_loop(0, n_loop, body, 0)
        out = acc_ref[...] / l_ref[...]

    o_ref[...] = jnp.concatenate([out[h * tq:(h + 1) * tq] for h in range(g)], axis=1).astype(o_ref.dtype)


def _prompt_attn(q, k, v, frow, mode, layer, b, s, kv, g, qcol):
    assert s & (s - 1) == 0
    tq = min(512, s)
    sub = min(256 if mode == "sb" else 512, tq)
    nq = s // tq
    gw = g * HEAD_DIM
    kv_spec = pl.BlockSpec((None, s, HEAD_DIM), lambda bi, j, i: (layer, bi, j))
    in_specs = [pl.BlockSpec((tq, gw), lambda bi, j, i: (bi * nq + i, qcol // gw + j)), kv_spec, kv_spec]
    args = [q, k, v]
    scratch = [pltpu.VMEM((g * tq, HEAD_DIM), F32)]
    if mode == "fox":
        for h in range(g):
            in_specs.append(pl.BlockSpec((None, 1, s), lambda bi, j, i, h=h: (j * g + h, 0, bi)))
            args.append(frow)
        scratch += [pltpu.VMEM((g * tq, LANES), F32), pltpu.VMEM((g * tq, LANES), F32)]
    return pl.pallas_call(
        functools.partial(_prompt_attn_kernel, mode=mode, g=g, tq=tq, sub=sub, cs=HEAD_DIM ** -0.5 * LOG2E),
        grid=(b, kv, nq),
        in_specs=in_specs,
        out_specs=pl.BlockSpec((tq, gw), lambda bi, j, i: (bi * nq + i, j)),
        out_shape=jax.ShapeDtypeStruct((b * s, kv * gw), BF16),
        scratch_shapes=scratch,
        compiler_params=_params("parallel", "parallel", "parallel"),
        name="prompt_" + mode,
    )(*args)


def _even_sample_kernel(pt_ref, qa_ref, qb_ref, kan_ref, van_ref, kbn_ref, vbn_ref, lfn_ref,
                        ka_hbm, va_hbm, kb_hbm, vb_hbm, lf_hbm, oa_ref, ob_ref,
                        kbuf, lfbuf, sem, acca_ref, c_ref, accb_ref, m_ref, l_ref, rc_ref,
                        *, layer, npg, cp, nb, kv, g, tt, page, cs):
    b = pl.program_id(0)
    nc = npg // cp
    total = pl.num_programs(0) * nc
    nh = kv * g
    rows = nh * tt
    gr = g * tt
    prow = page * kv
    caches = (ka_hbm, va_hbm, kb_hbm, vb_hbm)
    m_inc = _tri(page, lower_eq=True)
    m_inc2 = _tri(page, lower_eq=True, stack=2)

    def chunk_copies(slot, phys):
        out = []
        for u in range(cp):
            pg = phys(u)
            for a, cache in enumerate(caches):
                out.append(pltpu.make_async_copy(cache.at[layer, pg], kbuf.at[slot, a, pl.ds(u * prow, prow)], sem.at[slot]))
            out.append(pltpu.make_async_copy(lf_hbm.at[layer, pg], lfbuf.at[slot, pl.ds(u * nh, nh)], sem.at[slot]))
        return out

    def start_chunk(gid):
        row = gid // nc
        base = row * npg + npg - (gid - row * nc + 1) * cp
        for c in chunk_copies(gid % nb, lambda u: pt_ref[base + u]):
            c.start()

    @pl.when(b == 0)
    def _():
        for n in range(nb - 1):
            start_chunk(jnp.int32(n))

    def stack_heads(q):
        return jnp.concatenate([q[:, h * HEAD_DIM:(h + 1) * HEAD_DIM] for h in range(nh)], axis=0).astype(BF16)

    def rep_rows(a):
        return jnp.concatenate([jnp.broadcast_to(a[h:h + 1, :], (tt, a.shape[1])) for h in range(nh)], axis=0)

    def scores(q, kblocks, c):
        return jnp.concatenate(
            [_dot_nt(q[k * gr:(k + 1) * gr], kblocks[k]) for k in range(kv)], axis=0) * c

    def weighted(w, vblocks):
        wb = w.astype(BF16)
        return jnp.concatenate(
            [_dot(wb[k * gr:(k + 1) * gr], vblocks[k]) for k in range(kv)], axis=0)

    qa = stack_heads(qa_ref[...])
    qb = stack_heads(qb_ref[...])

    zpad = jnp.zeros((page - tt, HEAD_DIM), F32)

    def new_blocks(ref):
        x = ref[...]
        return [jnp.concatenate([x[:, k * HEAD_DIM:(k + 1) * HEAD_DIM], zpad], axis=0).astype(BF16)
                for k in range(kv)]

    t_of = lax.broadcasted_iota(jnp.int32, (rows, page), 0) & (tt - 1)
    col = lax.broadcasted_iota(jnp.int32, (rows, page), 1)
    mask = col < t_of
    ny = scores(qa, new_blocks(kan_ref), -cs)
    inc = _dot_split2(jnp.where(mask, _log2_sigmoid(ny), 0.0), m_inc2)
    acca_ref[...] = weighted(jnp.where(mask, jnp.exp2(inc - ny), 0.0), new_blocks(van_ref))
    c_ref[...] = jnp.broadcast_to(inc[:, 0:1], c_ref.shape)
    fnew = _dot_split3(lfn_ref[...], _tri(page, lower_eq=False))
    yb = scores(qb, new_blocks(kbn_ref), cs) - rep_rows(fnew) * LOG2E
    yb = jnp.where(col <= t_of, yb, NEG_INF)
    mx = jnp.max(yb, axis=1, keepdims=True)
    p = jnp.exp2(yb - mx)
    m_ref[...] = jnp.broadcast_to(mx, m_ref.shape)
    l_ref[...] = jnp.broadcast_to(jnp.sum(p, axis=1, keepdims=True), l_ref.shape)
    accb_ref[...] = weighted(p, new_blocks(vbn_ref))
    rc_ref[...] = jnp.zeros_like(rc_ref)

    def chunk(cc, carry):
        gid = b * nc + cc
        nxt = gid + nb - 1

        @pl.when(nxt < total)
        def _():
            start_chunk(nxt)

        slot = gid % nb
        for c in chunk_copies(slot, lambda u: 0):
            c.wait()

        def blocks(a):
            view = kbuf.at[slot, a]
            return [view[pl.ds(k, cp * page, stride=kv), :].astype(BF16) for k in range(kv)]

        ny = scores(qa, blocks(0), -cs)
        lk = _log2_sigmoid(ny)
        inc = _dot_split2(jnp.concatenate([lk[:, u * page:(u + 1) * page] for u in range(cp)], axis=0), m_inc2)
        c = c_ref[...]
        ws = [None] * cp
        for u in reversed(range(cp)):
            inc_u = inc[u * rows:(u + 1) * rows]
            ws[u] = jnp.exp2(c + inc_u - ny[:, u * page:(u + 1) * page])
            c = c + jnp.broadcast_to(inc_u[:, 0:1], c.shape)
        c_ref[...] = c
        acca_ref[...] += weighted(jnp.concatenate(ws, axis=1), blocks(1))

        lf = lfbuf[slot]
        incf = _dot_split3(lf, m_inc)
        rc = rc_ref[...]
        bias = [None] * cp
        for u in reversed(range(cp)):
            incf_u = incf[u * nh:(u + 1) * nh]
            bias[u] = rep_rows(rc + incf_u - lf[u * nh:(u + 1) * nh])
            rc = rc + jnp.broadcast_to(incf_u[:, 0:1], rc.shape)
        rc_ref[...] = rc
        yb = scores(qb, blocks(2), cs) + jnp.concatenate(bias, axis=1) * LOG2E
        m_prev = m_ref[...]
        m_new = jnp.maximum(m_prev, jnp.max(yb, axis=1, keepdims=True))
        alpha = jnp.exp2(m_prev - m_new)
        p = jnp.exp2(yb - _lane_tile(m_new, cp))
        l_ref[...] = alpha * l_ref[...] + jnp.sum(p, axis=1, keepdims=True)
        accb_ref[...] = alpha * accb_ref[...] + weighted(p, blocks(3))
        m_ref[...] = m_new
        return carry

    lax.fori_loop(0, nc, chunk, 0)

    def unstack(a):
        return jnp.concatenate([a[h * tt:(h + 1) * tt] for h in range(nh)], axis=1)

    oa_ref[...] = unstack(acca_ref[...])
    ob_ref[...] = unstack(accb_ref[...] / l_ref[...])


def _even_sample(q, knew, lfn, caches, lf_cache, page_table, layer, kv, g, tt):
    db, npg = page_table.shape
    page = caches[0].shape[2] // kv
    nh = kv * g
    cp = PAGES_PER_CHUNK if npg % PAGES_PER_CHUNK == 0 else 1
    nb = CHUNK_SLOTS
    assert db * (npg // cp) >= nb - 1
    qw = nh * HEAD_DIM
    kw = kv * HEAD_DIM
    new_spec = pl.BlockSpec((None, tt, kw), lambda b, pt: (layer, b, 0))
    in_specs = [pl.BlockSpec((tt, qw), lambda b, pt: (b, 0)), pl.BlockSpec((tt, qw), lambda b, pt: (b, 1)),
                new_spec, new_spec, new_spec, new_spec,
                pl.BlockSpec((None, nh, LANES), lambda b, pt: (b, 0, 0))]
    in_specs += [pl.BlockSpec(memory_space=pl.ANY)] * 5
    rows = nh * tt
    return pl.pallas_call(
        functools.partial(_even_sample_kernel, layer=layer, npg=npg, cp=cp, nb=nb, kv=kv, g=g, tt=tt,
                          page=page, cs=HEAD_DIM ** -0.5 * LOG2E),
        grid_spec=pltpu.PrefetchScalarGridSpec(
            num_scalar_prefetch=1,
            grid=(db,),
            in_specs=in_specs,
            out_specs=[pl.BlockSpec((tt, qw), lambda b, pt: (b, 0)), pl.BlockSpec((tt, qw), lambda b, pt: (b, 0))],
            scratch_shapes=[pltpu.VMEM((nb, 4, cp * page * kv, HEAD_DIM), F32),
                            pltpu.VMEM((nb, cp * nh, page), F32),
                            pltpu.SemaphoreType.DMA((nb,)),
                            pltpu.VMEM((rows, HEAD_DIM), F32), pltpu.VMEM((rows, LANES), F32),
                            pltpu.VMEM((rows, HEAD_DIM), F32), pltpu.VMEM((rows, LANES), F32),
                            pltpu.VMEM((rows, LANES), F32), pltpu.VMEM((nh, LANES), F32)],
        ),
        out_shape=[jax.ShapeDtypeStruct((db * tt, qw), F32), jax.ShapeDtypeStruct((db * tt, qw), F32)],
        compiler_params=_params("arbitrary"),
        name="even_sample",
    )(page_table.reshape(-1), q, q, *knew, lfn, *caches, lf_cache)


def _out_proj_kernel(a1_ref, a2_ref, w1_ref, w2_ref, x_ref, o_ref):
    o_ref[...] = (x_ref[...] + _dot(a1_ref[...].astype(BF16), w1_ref[...])
                  + _dot(a2_ref[...].astype(BF16), w2_ref[...]))


def _out_proj(a1, a2, col2, w, x, row0):
    n = a1.shape[0]
    d = x.shape[1]
    kh = w.shape[0] // 2
    tm = _pick_tile(n, 512, LANES)
    assert row0 % tm == 0
    r0 = row0 // tm
    return pl.pallas_call(
        _out_proj_kernel,
        grid=(n // tm,),
        in_specs=[
            pl.BlockSpec((tm, kh), lambda i: (i, 0)),
            pl.BlockSpec((tm, kh), lambda i: (i, col2)),
            pl.BlockSpec((kh, d), lambda i: (0, 0)),
            pl.BlockSpec((kh, d), lambda i: (1, 0)),
            pl.BlockSpec((tm, d), lambda i: (r0 + i, 0)),
        ],
        out_specs=pl.BlockSpec((tm, d), lambda i: (r0 + i, 0)),
        out_shape=jax.ShapeDtypeStruct(x.shape, F32),
        input_output_aliases={4: 0},
        compiler_params=_params("parallel"),
        name="out_proj",
    )(a1, a2, w, w, x)


def _rope_table_kernel(pos_ref, inv_ref, cos_ref, sina_ref, sinb_ref):
    ang = pos_ref[...] * inv_ref[...]
    lane = lax.broadcasted_iota(jnp.int32, ang.shape, 1)
    first = (lane & (HD_C - 1)) < HD_C // 2
    sin = jnp.sin(ang)
    cos_ref[...] = jnp.cos(ang)
    sina_ref[...] = jnp.where(first, -sin, 0.0)
    sinb_ref[...] = jnp.where(first, 0.0, sin)


def _rope_table(pos, inv):
    t = pos.shape[0]
    tm = _pick_tile(t, 768, LANES)
    return pl.pallas_call(
        _rope_table_kernel,
        grid=(t // tm,),
        in_specs=[pl.BlockSpec((tm, LANES), lambda i: (i, 0)), pl.BlockSpec((1, LANES), lambda i: (0, 0))],
        out_specs=[pl.BlockSpec((tm, LANES), lambda i: (i, 0))] * 3,
        out_shape=[jax.ShapeDtypeStruct((t, LANES), F32)] * 3,
        compiler_params=_params("parallel"),
        name="rope_table",
    )(pos, inv)


def _proj_odd_kernel(x_ref, g_ref, w_ref, cos_ref, sina_ref, sinb_ref, q_ref, kv_ref, h_ref, *, nq, kwc):
    j = pl.program_id(1)

    @pl.when(j == 0)
    def _():
        h_ref[...] = _rms(x_ref[...], g_ref[...]).astype(BF16)

    y = _dot(h_ref[...], w_ref[...].astype(BF16))
    tn = y.shape[1]
    half = HD_C // 2
    reps = tn // LANES
    roped = (y * _lane_tile(cos_ref[...], reps) + pltpu.roll(y, tn - half, 1) * _lane_tile(sina_ref[...], reps)
             + pltpu.roll(y, half, 1) * _lane_tile(sinb_ref[...], reps))

    @pl.when(j < nq)
    def _():
        q_ref[...] = roped

    @pl.when(j == nq)
    def _():
        kv_ref[:, :kwc] = roped[:, :kwc]
        kv_ref[:, kwc:] = y[:, kwc:]


def _proj_odd(x, row0, n, g, w_all, layer, cos_t, sin_t, qw, kwc):
    d = x.shape[1]
    tn = 2 * kwc
    assert qw % tn == 0 and w_all.shape[2] == qw + tn
    nq = qw // tn
    tm = _pick_tile(n, 1024, LANES)
    assert row0 % tm == 0
    r0 = row0 // tm
    return pl.pallas_call(
        functools.partial(_proj_odd_kernel, nq=nq, kwc=kwc),
        grid=(n // tm, nq + 1),
        in_specs=[
            pl.BlockSpec((tm, d), lambda i, j: (r0 + i, 0)),
            pl.BlockSpec((1, d), lambda i, j: (0, 0)),
            pl.BlockSpec((None, d, tn), lambda i, j: (layer, 0, j)),
        ] + [pl.BlockSpec((tm, LANES), lambda i, j: (r0 + i, 0))] * 3,
        out_specs=[pl.BlockSpec((tm, tn), lambda i, j: (i, jnp.minimum(j, nq - 1))),
                   pl.BlockSpec((tm, tn), lambda i, j: (i, 0))],
        out_shape=[jax.ShapeDtypeStruct((n, qw), F32), jax.ShapeDtypeStruct((n, tn), F32)],
        scratch_shapes=[pltpu.VMEM((tm, d), BF16)],
        compiler_params=_params("parallel", "arbitrary"),
        name="proj_odd",
    )(x, g.reshape(1, d), w_all, cos_t, *sin_t)


def _swa_group(q, kband, vband, mask, sink_col, g, cs):
    tq = q.shape[0]
    q8 = jnp.concatenate([q[:, h * HD_C:(h + 1) * HD_C] for h in range(g)], axis=0).astype(BF16)
    y = _dot_nt(q8, kband.astype(BF16)) * cs
    y = jnp.where(mask, y, NEG_INF)
    m = jnp.maximum(jnp.max(y, axis=1, keepdims=True), sink_col)
    p = jnp.exp2(y - m)
    denom = jnp.sum(p, axis=1, keepdims=True) + jnp.exp2(sink_col - m)
    o = _dot(p.astype(BF16), vband.astype(BF16)) / denom
    return jnp.concatenate([o[h * tq:(h + 1) * tq] for h in range(g)], axis=1)


def _sink_col(sinks_ref, k, g, tq):
    return jnp.concatenate([jnp.full((tq, 1), sinks_ref[k * g + h] * LOG2E, F32) for h in range(g)], axis=0)


def _pair_diag(ref_p, ref_c, col, odd):
    x = jnp.concatenate([ref_p[:, col * LANES:(col + 1) * LANES], ref_c[:, col * LANES:(col + 1) * LANES]], axis=0)
    low = lax.broadcasted_iota(jnp.int32, x.shape, 1) < HD_C
    xr = pltpu.roll(x, HD_C, 1)
    top = jnp.where(low, xr if odd else x, 0.0)
    bot = jnp.where(low, 0.0, x if odd else xr)
    return jnp.concatenate([top, bot], axis=0)


def _swa_prompt_kernel(sinks_ref, q_ref, kvc_ref, kvp_ref, o_ref, *, kv, g, cs):
    i = pl.program_id(1)
    tq = q_ref.shape[0]
    nk = 2 * tq
    npair = g // 2
    cols = npair * tq
    kcols = kv * HD_C // LANES
    kj = lax.broadcasted_iota(jnp.int32, (2 * nk, cols), 0) & (nk - 1)
    qi = lax.broadcasted_iota(jnp.int32, (2 * nk, cols), 1) & (tq - 1)
    bias = jnp.where((kj > qi) & (kj <= qi + tq) & ((kj >= tq) | (i > 0)), 0.0, NEG_INF)
    first = lax.broadcasted_iota(jnp.int32, (LANES, cols), 0) < HD_C
    for k in range(kv):
        kd = _pair_diag(kvp_ref, kvc_ref, k // 2, k % 2).astype(BF16)
        vdt = _pair_diag(kvp_ref, kvc_ref, kcols + k // 2, k % 2).T.astype(BF16)
        q4 = jnp.concatenate([q_ref[:, (k * npair + j) * LANES:(k * npair + j + 1) * LANES] for j in range(npair)],
                             axis=0).astype(BF16)
        y = _dot_nt(kd, q4) * cs + bias
        ps, dens = [], []
        for h in range(2):
            yh = y[h * nk:(h + 1) * nk]
            sink = jnp.concatenate([jnp.full((1, tq), sinks_ref[k * g + 2 * j + h] * LOG2E, F32) for j in range(npair)], axis=1)
            m = jnp.maximum(jnp.max(yh, axis=0, keepdims=True), sink)
            p = jnp.exp2(yh - m)
            ps.append(p.astype(BF16))
            dens.append(jnp.sum(p, axis=0, keepdims=True) + jnp.exp2(sink - m))
        ot = _dot(vdt, jnp.concatenate(ps, axis=0)) / jnp.where(first, dens[0], dens[1])
        o = ot.T
        for j in range(npair):
            c0 = (k * npair + j) * LANES
            o_ref[:, c0:c0 + LANES] = o[j * tq:(j + 1) * tq].astype(o_ref.dtype)


def _swa_prompt(q, kvn, sinks, b, s, kv, g):
    nq = s // WINDOW
    qw = kv * g * HD_C
    kw2 = 2 * kv * HD_C
    return pl.pallas_call(
        functools.partial(_swa_prompt_kernel, kv=kv, g=g, cs=HD_C ** -0.5 * LOG2E),
        grid_spec=pltpu.PrefetchScalarGridSpec(
            num_scalar_prefetch=1,
            grid=(b, nq),
            in_specs=[pl.BlockSpec((WINDOW, qw), lambda bi, i, sk: (bi * nq + i, 0)),
                      pl.BlockSpec((WINDOW, kw2), lambda bi, i, sk: (bi * nq + i, 0)),
                      pl.BlockSpec((WINDOW, kw2), lambda bi, i, sk: (jnp.maximum(bi * nq + i - 1, 0), 0))],
            out_specs=pl.BlockSpec((WINDOW, qw), lambda bi, i, sk: (bi * nq + i, 0)),
        ),
        out_shape=jax.ShapeDtypeStruct((b * s, qw), BF16),
        compiler_params=_params("parallel", "parallel"),
        name="swa_prompt",
    )(sinks, q, kvn, kvn)


def _swa_sample_kernel(sinks_ref, q_ref, kvn_ref, kbuf_ref, vbuf_ref, o_ref, *, kv, g, tt, cs):
    gw = g * HD_C
    kw = kv * HD_C
    nk = 2 * WINDOW
    t_of = lax.broadcasted_iota(jnp.int32, (g * tt, nk), 0) & (tt - 1)
    col = lax.broadcasted_iota(jnp.int32, (g * tt, nk), 1)
    mask = ((col < WINDOW) & (col > t_of)) | ((col >= WINDOW) & (col - WINDOW <= t_of))
    zpad = jnp.zeros((WINDOW - tt, HD_C), F32)
    for k in range(kv):
        ks = slice(k * HD_C, (k + 1) * HD_C)
        vs = slice(kw + k * HD_C, kw + (k + 1) * HD_C)
        kband = jnp.concatenate([kbuf_ref[:, ks], kvn_ref[:, ks], zpad], axis=0)
        vband = jnp.concatenate([vbuf_ref[:, ks], kvn_ref[:, vs], zpad], axis=0)
        o = _swa_group(q_ref[:, k * gw:(k + 1) * gw], kband, vband, mask, _sink_col(sinks_ref, k, g, tt), g, cs)
        o_ref[:, k * gw:(k + 1) * gw] = o


def _swa_sample(q, kvn, sinks, kbuf, vbuf, layer, kv, g, tt):
    db = kbuf.shape[1]
    qw = kv * g * HD_C
    kw = kv * HD_C
    buf_spec = pl.BlockSpec((None, None, WINDOW, kw), lambda b, sk: (layer, b, 0, 0))
    return pl.pallas_call(
        functools.partial(_swa_sample_kernel, kv=kv, g=g, tt=tt, cs=HD_C ** -0.5 * LOG2E),
        grid_spec=pltpu.PrefetchScalarGridSpec(
            num_scalar_prefetch=1,
            grid=(db,),
            in_specs=[pl.BlockSpec((tt, qw), lambda b, sk: (b, 0)),
                      pl.BlockSpec((tt, 2 * kw), lambda b, sk: (b, 0)),
                      buf_spec, buf_spec],
            out_specs=pl.BlockSpec((tt, qw), lambda b, sk: (b, 0)),
        ),
        out_shape=jax.ShapeDtypeStruct((db * tt, qw), F32),
        compiler_params=_params("parallel"),
        name="swa_sample",
    )(sinks, q, kvn, kbuf, vbuf)


def _final_norm_kernel(x_ref, g_ref, o_ref):
    o_ref[...] = _rms(x_ref[...], g_ref[...])


def _final_norm(x, row0, n, g):
    d = x.shape[1]
    tm = _pick_tile(n, 1024, LANES)
    assert row0 % tm == 0
    r0 = row0 // tm
    return pl.pallas_call(
        _final_norm_kernel,
        grid=(n // tm,),
        in_specs=[pl.BlockSpec((tm, d), lambda i: (r0 + i, 0)), pl.BlockSpec((1, d), lambda i: (0, 0))],
        out_specs=pl.BlockSpec((tm, d), lambda i: (i, 0)),
        out_shape=jax.ShapeDtypeStruct((n, d), F32),
        compiler_params=_params("parallel"),
        name="final_norm",
    )(x, g.reshape(1, d))


def kernel(x_prompt, x_sample, cache_sb_k, cache_sb_v, cache_fox_k, cache_fox_v, cache_fox_logf, cache_swa_k, cache_swa_v, page_table, norm_ffn1, norm_mix, norm_ffn2, norm_final, w_ffn_in, w_ffn_out, w_in_even, b_forget, w_out_even, w_in_odd, sinks, w_out_odd):
    b, s, d = x_prompt.shape
    db, tt, _ = x_sample.shape
    depth = norm_mix.shape[0]
    n_even, n_phys, page, kv_a, _ = cache_sb_k.shape
    n_odd = cache_swa_k.shape[0]
    kv_c = cache_swa_k.shape[3]
    h_b = cache_fox_logf.shape[3]
    kv_b = cache_fox_k.shape[3]
    h_a = (w_in_even.shape[2] - h_b - 2 * (kv_a + kv_b) * HEAD_DIM) // HEAD_DIM - h_b
    g_a, g_b = h_a // kv_a, h_b // kv_b
    h_c = w_out_odd.shape[1] // HD_C
    g_c = h_c // kv_c
    assert kv_a == kv_b and g_a == g_b and h_b == SUBLANES and tt == SUBLANES
    bs = b * s
    ns = db * tt
    past_len = page_table.shape[1] * page
    kw = kv_a * HEAD_DIM
    qwa = h_a * HEAD_DIM
    qw = qwa + h_b * HEAD_DIM

    x = jnp.concatenate([x_prompt.reshape(bs, d), x_sample.reshape(ns, d)], axis=0)
    w_even_t = jnp.swapaxes(w_in_even, 1, 2)
    w_in_b = w_ffn_in[0, 0].astype(BF16)
    w_out_b = w_ffn_out[0, 0].astype(BF16)

    paged = [c.reshape(n_even, n_phys, page * kv_a, HEAD_DIM) for c in (cache_sb_k, cache_sb_v, cache_fox_k, cache_fox_v)]
    lf_cache = jnp.swapaxes(cache_fox_logf, 2, 3)
    swa_kbuf = cache_swa_k.reshape(n_odd, db, WINDOW, kv_c * HD_C)
    swa_vbuf = cache_swa_v.reshape(n_odd, db, WINDOW, kv_c * HD_C)

    half = HD_C // 2
    inv_freq = ROPE_THETA ** (-jnp.arange(half, dtype=F32) / half)
    inv = jnp.tile(inv_freq, LANES // half).reshape(1, LANES)
    pos = jnp.concatenate([jnp.tile(jnp.arange(s, dtype=jnp.int32), b),
                           jnp.tile(past_len + jnp.arange(tt, dtype=jnp.int32), db)]).astype(F32)
    cos_t, *sin_t = _rope_table(jnp.broadcast_to(pos[:, None], (bs + ns, LANES)), inv)

    kv_p = [jnp.zeros((n_even, bs, kw), F32) for _ in range(4)]
    kv_s = [jnp.zeros((n_even, ns, kw), F32) for _ in range(4)]
    lf_p, lf_s = [], []
    odd_p = [[], []]
    odd_s = [[], []]

    for l in range(depth):
        x, w_in_b, w_out_b = _ffn(x, norm_ffn1[l], w_in_b, w_out_b, (w_ffn_in, w_ffn_out, l, 1))
        i = l // 2
        if l % 2 == 0:
            w_f = jnp.pad(w_even_t[i, qw + 4 * kw:, :], ((0, LANES - h_b), (0, 0)))
            b_f = jnp.pad(b_forget[i], (0, LANES - h_b)).reshape(1, LANES)
            q_p, kv_p, lfp, lft_p = _proj_even(x, 0, bs, norm_mix[l], w_even_t, w_f, b_f, qwa, qw, kw, i, n_even, kv_p)
            q_s, kv_s, lfs, lft_s = _proj_even(x, bs, ns, norm_mix[l], w_even_t, w_f, b_f, qwa, qw, kw, i, n_even, kv_s)
            lf_p.append(lfp[:, :h_b].reshape(b, s, h_b))
            lf_s.append(lfs[:, :h_b].reshape(db, tt, h_b))
            frow = _fcum(lft_p, b, s).reshape(h_b, 1, bs)
            oa_p = _prompt_attn(q_p, kv_p[0], kv_p[1], None, "sb", i, b, s, kv_a, g_a, 0)
            ob_p = _prompt_attn(q_p, kv_p[2], kv_p[3], frow, "fox", i, b, s, kv_b, g_b, qwa)
            lfn = jnp.pad(lft_s.reshape(h_b, db, tt).transpose(1, 0, 2), ((0, 0), (0, 0), (0, LANES - tt)))
            oa_s, ob_s = _even_sample(q_s, kv_s, lfn, paged, lf_cache, page_table, i, kv_a, g_a, tt)
            w_out = w_out_even[i].astype(BF16)
            x = _out_proj(oa_p, ob_p, 0, w_out, x, 0)
            x = _out_proj(oa_s, ob_s, 0, w_out, x, bs)
        else:
            qwc = h_c * HD_C
            kwc = kv_c * HD_C
            q_p, kvn_p = _proj_odd(x, 0, bs, norm_mix[l], w_in_odd, i, cos_t, sin_t, qwc, kwc)
            q_s, kvn_s = _proj_odd(x, bs, ns, norm_mix[l], w_in_odd, i, cos_t, sin_t, qwc, kwc)
            o_p = _swa_prompt(q_p, kvn_p, sinks[i], b, s, kv_c, g_c)
            o_s = _swa_sample(q_s, kvn_s, sinks[i], swa_kbuf, swa_vbuf, i, kv_c, g_c, tt)
            w_out = w_out_odd[i].astype(BF16)
            x = _out_proj(o_p, o_p, 1, w_out, x, 0)
            x = _out_proj(o_s, o_s, 1, w_out, x, bs)
            for n, buf in enumerate((cache_swa_k[i], cache_swa_v[i])):
                cols = slice(n * kwc, (n + 1) * kwc)
                odd_p[n].append(kvn_p.reshape(b, s, 2 * kwc)[:, s - WINDOW:, cols].reshape(b, WINDOW, kv_c, HD_C))
                odd_s[n].append(jnp.concatenate([buf[:, tt:], kvn_s[:, cols].reshape(db, tt, kv_c, HD_C)], axis=1))
        x, w_in_b, w_out_b = _ffn(x, norm_ffn2[l], w_in_b, w_out_b,
                                  (w_ffn_in, w_ffn_out, l + 1, 0) if l + 1 < depth else None)

    outs = [_final_norm(x, 0, bs, norm_final).reshape(b, s, d), _final_norm(x, bs, ns, norm_final).reshape(db, tt, d)]
    outs += [a.reshape(n_even, b, s, kv_a, HEAD_DIM) for a in kv_p] + [jnp.stack(lf_p)]
    outs += [jnp.stack(a) for a in odd_p]
    outs += [a.reshape(n_even, db, tt, kv_a, HEAD_DIM) for a in kv_s] + [jnp.stack(lf_s)]
    outs += [jnp.stack(a) for a in odd_s]
    return tuple(outs)
```

```python
import functools
import math

import jax
import jax.numpy as jnp
from jax import lax
from jax.experimental import pallas as pl
from jax.experimental.pallas import tpu as pltpu

F32 = jnp.float32
BF16 = jnp.bfloat16

HEAD_DIM = 128
HD_C = 64
WINDOW = 128
ROPE_THETA = 10000.0
EPS = 1e-6
NEG_INF = -1e30
LOG2E = math.log2(math.e)
LANES = 128
SUBLANES = 8
VMEM_LIMIT = 56 * 1024 * 1024
PAGES_PER_CHUNK = 8
CHUNK_SLOTS = 3


def _pick_tile(n, target, mult):
    best = None
    for t in range(mult, min(n, target) + 1, mult):
        if n % t == 0:
            best = t
    assert best is not None, (n, target, mult)
    return best


def _params(*sem):
    return pltpu.CompilerParams(dimension_semantics=sem, vmem_limit_bytes=VMEM_LIMIT)


def _dot(a, b):
    return jnp.dot(a, b, preferred_element_type=F32)


def _dot_nt(a, b):
    return lax.dot_general(a, b, (((1,), (1,)), ((), ())), preferred_element_type=F32)


def _rms(x, g):
    ms = jnp.mean(x * x, axis=-1, keepdims=True)
    return x * lax.rsqrt(ms + EPS) * g


def _log_sigmoid(x):
    return jnp.minimum(x, 0.0) - jnp.log1p(jnp.exp(-jnp.abs(x)))


def _log2_sigmoid(ny):
    neg_abs = lax.bitcast_convert_type(lax.bitcast_convert_type(ny, jnp.uint32) | jnp.uint32(0x80000000), F32)
    return jnp.minimum(ny, 0.0) - jnp.log2(1.0 + jnp.exp2(neg_abs))


def _split2(a):
    hi = a.astype(BF16)
    lo = (a - hi.astype(F32)).astype(BF16)
    return hi, lo


def _dot_split2(a, m01x2):
    return _dot(jnp.concatenate(_split2(a), axis=1), m01x2)


def _dot_split3(a, m01):
    a1 = a.astype(BF16)
    r1 = a - a1.astype(F32)
    a2 = r1.astype(BF16)
    a3 = (r1 - a2.astype(F32)).astype(BF16)
    return _dot(a1, m01) + _dot(a2, m01) + _dot(a3, m01)


def _tri(n, lower_eq, stack=1):
    r = lax.broadcasted_iota(jnp.int32, (stack * n, n), 0) & (n - 1)
    c = lax.broadcasted_iota(jnp.int32, (stack * n, n), 1)
    m = (r >= c) if lower_eq else (r <= c)
    return jnp.where(m, 1.0, 0.0).astype(BF16)


def _lane_tile(a, reps):
    return a if reps == 1 else jnp.concatenate([a] * reps, axis=1)


def _ffn_kernel(x_ref, g_ref, wg_ref, wu_ref, wo_ref, *rest, convert_next):
    if convert_next:
        nin_ref, nout_ref, o_ref, cin_ref, cout_ref, h_ref = rest
        cin_ref[...] = nin_ref[...].astype(BF16)
        cout_ref[...] = nout_ref[...].astype(BF16)
    else:
        o_ref, h_ref = rest

    @pl.when(pl.program_id(1) == 0)
    def _():
        x = x_ref[...]
        h_ref[...] = _rms(x, g_ref[...]).astype(BF16)
        o_ref[...] = x

    h = h_ref[...]
    gate = _dot(h, wg_ref[...])
    up = _dot(h, wu_ref[...])
    a = (gate * jax.nn.sigmoid(gate)) * (0.5 * up)
    o_ref[...] += _dot(a.astype(BF16), wo_ref[...])


def _pow2_blocks(n, limit, mult):
    k = 1
    while 2 * k <= limit and n % (2 * k) == 0 and (n // (2 * k)) % mult == 0:
        k *= 2
    return k


def _ffn(x, g, w_in, w_out, nxt=None):
    t, d = x.shape
    f = w_out.shape[0]
    tm = _pick_tile(t, 768, LANES)
    tf = _pick_tile(f, 512, LANES)
    nf = f // tf
    ni = t // tm
    in_specs = [
        pl.BlockSpec((tm, d), lambda i, j: (i, 0)),
        pl.BlockSpec((1, d), lambda i, j: (0, 0)),
        pl.BlockSpec((d, tf), lambda i, j: (0, j)),
        pl.BlockSpec((d, tf), lambda i, j: (0, nf + j)),
        pl.BlockSpec((tf, d), lambda i, j: (j, 0)),
    ]
    args = [x, g.reshape(1, d), w_in, w_in, w_out]
    out_specs = [pl.BlockSpec((tm, d), lambda i, j: (i, 0))]
    out_shape = [jax.ShapeDtypeStruct((t, d), F32)]
    if nxt is not None:
        w_in_all, w_out_all, ln, hn = nxt
        nr = _pow2_blocks(d, ni, 2 * SUBLANES)
        ncol = _pow2_blocks(d, ni, LANES)
        rb, cb = d // nr, 2 * f // nf
        rb2, cb2 = f // nf, d // ncol

        def in_idx(i, j):
            return jnp.minimum(i, nr - 1), jnp.where(i < nr, j, nf - 1)

        def out_idx(i, j):
            return jnp.where(i < ncol, j, nf - 1), jnp.minimum(i, ncol - 1)

        in_specs += [pl.BlockSpec((None, None, rb, cb), lambda i, j: (ln, hn, *in_idx(i, j))),
                     pl.BlockSpec((None, None, rb2, cb2), lambda i, j: (ln, hn, *out_idx(i, j)))]
        args += [w_in_all, w_out_all]
        out_specs += [pl.BlockSpec((rb, cb), in_idx), pl.BlockSpec((rb2, cb2), out_idx)]
        out_shape += [jax.ShapeDtypeStruct((d, 2 * f), BF16), jax.ShapeDtypeStruct((f, d), BF16)]
    outs = pl.pallas_call(
        functools.partial(_ffn_kernel, convert_next=nxt is not None),
        grid=(ni, nf),
        in_specs=in_specs,
        out_specs=out_specs,
        out_shape=out_shape,
        scratch_shapes=[pltpu.VMEM((tm, d), BF16)],
        compiler_params=_params("arbitrary", "arbitrary"),
        name="ffn",
    )(*args)
    return outs if nxt is not None else (outs[0], None, None)


def _proj_even_kernel(*refs, nq, nkv):
    x_ref, g_ref, w_ref, wf_ref, bf_ref = refs[:5]
    outs = refs[5 + nkv:]
    q_ref, kv_refs, lf_ref, lft_ref, h_ref = outs[0], outs[1:1 + nkv], outs[1 + nkv], outs[2 + nkv], outs[3 + nkv]
    j = pl.program_id(1)

    @pl.when(j == 0)
    def _():
        hb = _rms(x_ref[...], g_ref[...]).astype(BF16)
        h_ref[...] = hb
        lf = _log_sigmoid(_dot_nt(hb, wf_ref[...].astype(BF16)) + bf_ref[...])
        lf_ref[...] = lf
        lft_ref[...] = lf.T[:SUBLANES, :]

    y = _dot_nt(h_ref[...], w_ref[...].astype(BF16))

    @pl.when(j < nq)
    def _():
        q_ref[...] = y

    for n, ref in enumerate(kv_refs):
        @pl.when(j == nq + n)
        def _(ref=ref):
            ref[...] = y


def _proj_even(x, row0, n, g, w_all, w_f, b_f, qwa, qw, kw, layer, n_layers, prev):
    d = x.shape[1]
    tm = _pick_tile(n, 1024, LANES)
    assert row0 % tm == 0 and qw % kw == 0 and qwa % kw == 0
    r0 = row0 // tm
    nq = qw // kw
    nqa = qwa // kw
    nkv = len(prev)

    def wcol(j):
        return jnp.where(j < nqa, j, jnp.where(j < nq, j + 2, jnp.where(j < nq + 2, j - (nq - nqa), j)))

    in_specs = [
        pl.BlockSpec((tm, d), lambda i, j: (r0 + i, 0)),
        pl.BlockSpec((1, d), lambda i, j: (0, 0)),
        pl.BlockSpec((None, kw, d), lambda i, j: (layer, wcol(j), 0)),
        pl.BlockSpec((LANES, d), lambda i, j: (0, 0)),
        pl.BlockSpec((1, LANES), lambda i, j: (0, 0)),
    ]
    args = [x, g.reshape(1, d), w_all, w_f, b_f]
    aliases = {}
    for m, arr in enumerate(prev):
        in_specs.append(pl.BlockSpec(memory_space=pl.ANY))
        args.append(arr)
        aliases[5 + m] = 1 + m
    out_specs = [pl.BlockSpec((tm, kw), lambda i, j: (i, jnp.minimum(j, nq - 1)))]
    out_shape = [jax.ShapeDtypeStruct((n, qw), F32)]
    for _ in range(nkv):
        out_specs.append(pl.BlockSpec((None, tm, kw), lambda i, j: (layer, i, 0)))
        out_shape.append(jax.ShapeDtypeStruct((n_layers, n, kw), F32))
    out_specs += [pl.BlockSpec((tm, LANES), lambda i, j: (i, 0)), pl.BlockSpec((SUBLANES, tm), lambda i, j: (0, i))]
    out_shape += [jax.ShapeDtypeStruct((n, LANES), F32), jax.ShapeDtypeStruct((SUBLANES, n), F32)]
    outs = pl.pallas_call(
        functools.partial(_proj_even_kernel, nq=nq, nkv=nkv),
        grid=(n // tm, nq + nkv),
        in_specs=in_specs,
        out_specs=out_specs,
        out_shape=out_shape,
        scratch_shapes=[pltpu.VMEM((tm, d), BF16)],
        input_output_aliases=aliases,
        compiler_params=_params("parallel", "arbitrary"),
        name="proj_even",
    )(*args)
    return outs[0], list(outs[1:1 + nkv]), outs[1 + nkv], outs[2 + nkv]


def _fcum_kernel(lft_ref, o_ref, carry_ref):
    @pl.when(pl.program_id(1) == 0)
    def _():
        carry_ref[...] = jnp.zeros_like(carry_ref)

    blk = lft_ref.shape[1]
    cum = _dot_split3(lft_ref[...], _tri(blk, lower_eq=False)) + carry_ref[:, 0:1]
    o_ref[...] = cum
    carry_ref[...] = jnp.broadcast_to(cum[:, blk - 1:blk], carry_ref.shape)


def _fcum(lft, b, s):
    blk = _pick_tile(s, 512, LANES)
    nb = s // blk
    return pl.pallas_call(
        _fcum_kernel,
        grid=(b, nb),
        in_specs=[pl.BlockSpec((SUBLANES, blk), lambda i, j: (0, i * nb + j))],
        out_specs=pl.BlockSpec((SUBLANES, blk), lambda i, j: (0, i * nb + j)),
        out_shape=jax.ShapeDtypeStruct((SUBLANES, b * s), F32),
        scratch_shapes=[pltpu.VMEM((SUBLANES, LANES), F32)],
        compiler_params=_params("parallel", "arbitrary"),
        name="fcum",
    )(lft)


def _block_scores(q2, k_ref, start, sub, c):
    if not isinstance(start, int):
        start = pl.multiple_of(start, sub)
    return _dot_nt(q2, k_ref[pl.ds(start, sub), :].astype(BF16)) * c


def _sb_block(ny, v_ref, start, sub, mask, m_inc2, c, acc_ref):
    vs = v_ref[pl.ds(pl.multiple_of(start, sub), sub), :].astype(BF16)
    lk = _log2_sigmoid(ny)
    if mask is not None:
        lk = jnp.where(mask, lk, 0.0)
    inc = _dot_split2(lk, m_inc2)
    w = jnp.exp2(_lane_tile(c, sub // LANES) + inc - ny)
    if mask is not None:
        w = jnp.where(mask, w, 0.0)
    acc_ref[...] += _dot(w.astype(BF16), vs)
    return c + jnp.broadcast_to(inc[:, 0:1], c.shape)


def _fox_block(y, v_ref, fk, start, sub, mask, m_ref, l_ref, acc_ref):
    vs = v_ref[pl.ds(pl.multiple_of(start, sub), sub), :].astype(BF16)
    y = y - fk
    if mask is not None:
        y = jnp.where(mask, y, NEG_INF)
    m_prev = m_ref[...]
    m_new = jnp.maximum(m_prev, jnp.max(y, axis=1, keepdims=True))
    alpha = jnp.exp2(m_prev - m_new)
    p = jnp.exp2(y - _lane_tile(m_new, sub // LANES))
    l_ref[...] = alpha * l_ref[...] + jnp.sum(p, axis=1, keepdims=True)
    acc_ref[...] = alpha * acc_ref[...] + _dot(p.astype(BF16), vs)
    m_ref[...] = m_new


def _prompt_attn_kernel(*refs, mode, g, tq, sub, cs):
    if mode == "sb":
        q_ref, k_ref, v_ref, o_ref, acc_ref = refs
    else:
        q_ref, k_ref, v_ref, *fk_refs, o_ref, acc_ref, m_ref, l_ref = refs
    i = pl.program_id(2)
    nd = tq // sub
    rows = g * tq
    q = q_ref[...]
    q2 = jnp.concatenate([q[:, h * HEAD_DIM:(h + 1) * HEAD_DIM] for h in range(g)], axis=0).astype(BF16)
    qi = lax.broadcasted_iota(jnp.int32, (rows, sub), 0) & (tq - 1)
    kj = lax.broadcasted_iota(jnp.int32, (rows, sub), 1)
    acc_ref[...] = jnp.zeros_like(acc_ref)

    n_loop = i * nd
    if mode == "sb":
        m_inc = _tri(sub, lower_eq=True, stack=2)
        c = jnp.zeros((rows, LANES), F32)
        ny = _block_scores(q2, k_ref, i * tq + (nd - 1) * sub, sub, -cs)
        for d in reversed(range(nd)):
            start = i * tq + d * sub
            ny_next = _block_scores(q2, k_ref, jnp.maximum(start - sub, 0), sub, -cs)
            c = _sb_block(ny, v_ref, start, sub, kj + d * sub < qi, m_inc, c, acc_ref)
            ny = ny_next

        def body(jb, carry):
            c, ny = carry
            for d in reversed(range(nd)):
                start = (i - 1 - jb) * tq + d * sub
                ny_next = _block_scores(q2, k_ref, jnp.maximum(start - sub, 0), sub, -cs)
                c = _sb_block(ny, v_ref, start, sub, None, m_inc, c, acc_ref)
                ny = ny_next
            return c, ny

        lax.fori_loop(0, i, body, (c, ny))
        out = acc_ref[...]
    else:
        m_ref[...] = jnp.full_like(m_ref, NEG_INF)
        l_ref[...] = jnp.zeros_like(l_ref)

        def fk_rows(start):
            return jnp.concatenate(
                [jnp.broadcast_to(r[:, pl.ds(pl.multiple_of(start, sub), sub)] * LOG2E, (tq, sub)) for r in fk_refs],
                axis=0)

        for d in range(nd):
            start = i * tq + d * sub
            _fox_block(_block_scores(q2, k_ref, start, sub, cs), v_ref, fk_rows(start), start, sub,
                       kj + d * sub <= qi, m_ref, l_ref, acc_ref)

        def body(jj, carry):
            start = jj * sub
            _fox_block(_block_scores(q2, k_ref, start, sub, cs), v_ref, fk_rows(start), start, sub,
                       None, m_ref, l_ref, acc_ref)
            return carry

        lax.fori_loop(0, n_loop, body, 0)
        out = acc_ref[...] / l_ref[...]

    o_ref[...] = jnp.concatenate([out[h * tq:(h + 1) * tq] for h in range(g)], axis=1).astype(o_ref.dtype)


def _prompt_attn(q, k, v, frow, mode, layer, b, s, kv, g, qcol):
    assert s & (s - 1) == 0
    tq = min(512, s)
    sub = min(256 if mode == "sb" else 512, tq)
    nq = s // tq
    gw = g * HEAD_DIM
    kv_spec = pl.BlockSpec((None, s, HEAD_DIM), lambda bi, j, i: (layer, bi, j))
    in_specs = [pl.BlockSpec((tq, gw), lambda bi, j, i: (bi * nq + i, qcol // gw + j)), kv_spec, kv_spec]
    args = [q, k, v]
    scratch = [pltpu.VMEM((g * tq, HEAD_DIM), F32)]
    if mode == "fox":
        for h in range(g):
            in_specs.append(pl.BlockSpec((None, 1, s), lambda bi, j, i, h=h: (j * g + h, 0, bi)))
            args.append(frow)
        scratch += [pltpu.VMEM((g * tq, LANES), F32), pltpu.VMEM((g * tq, LANES), F32)]
    return pl.pallas_call(
        functools.partial(_prompt_attn_kernel, mode=mode, g=g, tq=tq, sub=sub, cs=HEAD_DIM ** -0.5 * LOG2E),
        grid=(b, kv, nq),
        in_specs=in_specs,
        out_specs=pl.BlockSpec((tq, gw), lambda bi, j, i: (bi * nq + i, j)),
        out_shape=jax.ShapeDtypeStruct((b * s, kv * gw), BF16),
        scratch_shapes=scratch,
        compiler_params=_params("parallel", "parallel", "parallel"),
        name="prompt_" + mode,
    )(*args)


def _even_sample_kernel(pt_ref, qa_ref, qb_ref, kan_ref, van_ref, kbn_ref, vbn_ref, lfn_ref,
                        ka_hbm, va_hbm, kb_hbm, vb_hbm, lf_hbm, oa_ref, ob_ref,
                        kbuf, lfbuf, sem, acca_ref, c_ref, accb_ref, m_ref, l_ref, rc_ref,
                        *, layer, npg, cp, nb, kv, g, tt, page, cs):
    b = pl.program_id(0)
    nc = npg // cp
    total = pl.num_programs(0) * nc
    nh = kv * g
    rows = nh * tt
    gr = g * tt
    prow = page * kv
    caches = (ka_hbm, va_hbm, kb_hbm, vb_hbm)
    m_inc = _tri(page, lower_eq=True)
    m_inc2 = _tri(page, lower_eq=True, stack=2)

    def chunk_copies(slot, phys):
        out = []
        for u in range(cp):
            pg = phys(u)
            for a, cache in enumerate(caches):
                out.append(pltpu.make_async_copy(cache.at[layer, pg], kbuf.at[slot, a, pl.ds(u * prow, prow)], sem.at[slot]))
            out.append(pltpu.make_async_copy(lf_hbm.at[layer, pg], lfbuf.at[slot, pl.ds(u * nh, nh)], sem.at[slot]))
        return out

    def start_chunk(gid):
        row = gid // nc
        base = row * npg + npg - (gid - row * nc + 1) * cp
        for n, c in enumerate(chunk_copies(gid % nb, lambda u: pt_ref[base + u])):
            c.start(priority=n % 2)

    @pl.when(b == 0)
    def _():
        for n in range(nb - 1):
            start_chunk(jnp.int32(n))

    def stack_heads(q):
        return jnp.concatenate([q[:, h * HEAD_DIM:(h + 1) * HEAD_DIM] for h in range(nh)], axis=0).astype(BF16)

    def rep_rows(a):
        return jnp.concatenate([jnp.broadcast_to(a[h:h + 1, :], (tt, a.shape[1])) for h in range(nh)], axis=0)

    def scores(q, kblocks, c):
        return jnp.concatenate(
            [_dot_nt(q[k * gr:(k + 1) * gr], kblocks[k]) for k in range(kv)], axis=0) * c

    def weighted(w, vblocks):
        wb = w.astype(BF16)
        return jnp.concatenate(
            [_dot(wb[k * gr:(k + 1) * gr], vblocks[k]) for k in range(kv)], axis=0)

    qa = stack_heads(qa_ref[...])
    qb = stack_heads(qb_ref[...])

    zpad = jnp.zeros((page - tt, HEAD_DIM), F32)

    def new_blocks(ref):
        x = ref[...]
        return [jnp.concatenate([x[:, k * HEAD_DIM:(k + 1) * HEAD_DIM], zpad], axis=0).astype(BF16)
                for k in range(kv)]

    t_of = lax.broadcasted_iota(jnp.int32, (rows, page), 0) & (tt - 1)
    col = lax.broadcasted_iota(jnp.int32, (rows, page), 1)
    mask = col < t_of
    ny = scores(qa, new_blocks(kan_ref), -cs)
    inc = _dot_split2(jnp.where(mask, _log2_sigmoid(ny), 0.0), m_inc2)
    acca_ref[...] = weighted(jnp.where(mask, jnp.exp2(inc - ny), 0.0), new_blocks(van_ref))
    c_ref[...] = jnp.broadcast_to(inc[:, 0:1], c_ref.shape)
    fnew = _dot_split3(lfn_ref[...], _tri(page, lower_eq=False))
    yb = scores(qb, new_blocks(kbn_ref), cs) - rep_rows(fnew) * LOG2E
    yb = jnp.where(col <= t_of, yb, NEG_INF)
    mx = jnp.max(yb, axis=1, keepdims=True)
    p = jnp.exp2(yb - mx)
    m_ref[...] = jnp.broadcast_to(mx, m_ref.shape)
    l_ref[...] = jnp.broadcast_to(jnp.sum(p, axis=1, keepdims=True), l_ref.shape)
    accb_ref[...] = weighted(p, new_blocks(vbn_ref))
    rc_ref[...] = jnp.zeros_like(rc_ref)

    def chunk(cc, carry):
        gid = b * nc + cc
        nxt = gid + nb - 1

        @pl.when(nxt < total)
        def _():
            start_chunk(nxt)

        slot = gid % nb
        for c in chunk_copies(slot, lambda u: 0):
            c.wait()

        def blocks(a):
            view = kbuf.at[slot, a]
            return [view[pl.ds(k, cp * page, stride=kv), :].astype(BF16) for k in range(kv)]

        ny = scores(qa, blocks(0), -cs)
        lk = _log2_sigmoid(ny)
        inc = _dot_split2(jnp.concatenate([lk[:, u * page:(u + 1) * page] for u in range(cp)], axis=0), m_inc2)
        c = c_ref[...]
        ws = [None] * cp
        for u in reversed(range(cp)):
            inc_u = inc[u * rows:(u + 1) * rows]
            ws[u] = jnp.exp2(c + inc_u - ny[:, u * page:(u + 1) * page])
            c = c + jnp.broadcast_to(inc_u[:, 0:1], c.shape)
        c_ref[...] = c
        acca_ref[...] += weighted(jnp.concatenate(ws, axis=1), blocks(1))

        lf = lfbuf[slot]
        incf = _dot_split3(lf, m_inc)
        rc = rc_ref[...]
        bias = [None] * cp
        for u in reversed(range(cp)):
            incf_u = incf[u * nh:(u + 1) * nh]
            bias[u] = rep_rows(rc + incf_u - lf[u * nh:(u + 1) * nh])
            rc = rc + jnp.broadcast_to(incf_u[:, 0:1], rc.shape)
        rc_ref[...] = rc
        yb = scores(qb, blocks(2), cs) + jnp.concatenate(bias, axis=1) * LOG2E
        m_prev = m_ref[...]
        m_new = jnp.maximum(m_prev, jnp.max(yb, axis=1, keepdims=True))
        alpha = jnp.exp2(m_prev - m_new)
        p = jnp.exp2(yb - _lane_tile(m_new, cp))
        l_ref[...] = alpha * l_ref[...] + jnp.sum(p, axis=1, keepdims=True)
        accb_ref[...] = alpha * accb_ref[...] + weighted(p, blocks(3))
        m_ref[...] = m_new
        return carry

    lax.fori_loop(0, nc, chunk, 0)

    def unstack(a):
        return jnp.concatenate([a[h * tt:(h + 1) * tt] for h in range(nh)], axis=1)

    oa_ref[...] = unstack(acca_ref[...])
    ob_ref[...] = unstack(accb_ref[...] / l_ref[...])


def _even_sample(q, knew, lfn, caches, lf_cache, page_table, layer, kv, g, tt):
    db, npg = page_table.shape
    page = caches[0].shape[2] // kv
    nh = kv * g
    cp = PAGES_PER_CHUNK if npg % PAGES_PER_CHUNK == 0 else 1
    nb = CHUNK_SLOTS
    assert db * (npg // cp) >= nb - 1
    qw = nh * HEAD_DIM
    kw = kv * HEAD_DIM
    new_spec = pl.BlockSpec((None, tt, kw), lambda b, pt: (layer, b, 0))
    in_specs = [pl.BlockSpec((tt, qw), lambda b, pt: (b, 0)), pl.BlockSpec((tt, qw), lambda b, pt: (b, 1)),
                new_spec, new_spec, new_spec, new_spec,
                pl.BlockSpec((None, nh, LANES), lambda b, pt: (b, 0, 0))]
    in_specs += [pl.BlockSpec(memory_space=pl.ANY)] * 5
    rows = nh * tt
    return pl.pallas_call(
        functools.partial(_even_sample_kernel, layer=layer, npg=npg, cp=cp, nb=nb, kv=kv, g=g, tt=tt,
                          page=page, cs=HEAD_DIM ** -0.5 * LOG2E),
        grid_spec=pltpu.PrefetchScalarGridSpec(
            num_scalar_prefetch=1,
            grid=(db,),
            in_specs=in_specs,
            out_specs=[pl.BlockSpec((tt, qw), lambda b, pt: (b, 0)), pl.BlockSpec((tt, qw), lambda b, pt: (b, 0))],
            scratch_shapes=[pltpu.VMEM((nb, 4, cp * page * kv, HEAD_DIM), F32),
                            pltpu.VMEM((nb, cp * nh, page), F32),
                            pltpu.SemaphoreType.DMA((nb,)),
                            pltpu.VMEM((rows, HEAD_DIM), F32), pltpu.VMEM((rows, LANES), F32),
                            pltpu.VMEM((rows, HEAD_DIM), F32), pltpu.VMEM((rows, LANES), F32),
                            pltpu.VMEM((rows, LANES), F32), pltpu.VMEM((nh, LANES), F32)],
        ),
        out_shape=[jax.ShapeDtypeStruct((db * tt, qw), F32), jax.ShapeDtypeStruct((db * tt, qw), F32)],
        compiler_params=_params("arbitrary"),
        name="even_sample",
    )(page_table.reshape(-1), q, q, *knew, lfn, *caches, lf_cache)


def _out_proj_kernel(a1_ref, a2_ref, w1_ref, w2_ref, x_ref, o_ref):
    o_ref[...] = (x_ref[...] + _dot(a1_ref[...].astype(BF16), w1_ref[...])
                  + _dot(a2_ref[...].astype(BF16), w2_ref[...]))


def _out_proj(a1, a2, col2, w, x, row0):
    n = a1.shape[0]
    d = x.shape[1]
    kh = w.shape[0] // 2
    tm = _pick_tile(n, 512, LANES)
    assert row0 % tm == 0
    r0 = row0 // tm
    return pl.pallas_call(
        _out_proj_kernel,
        grid=(n // tm,),
        in_specs=[
            pl.BlockSpec((tm, kh), lambda i: (i, 0)),
            pl.BlockSpec((tm, kh), lambda i: (i, col2)),
            pl.BlockSpec((kh, d), lambda i: (0, 0)),
            pl.BlockSpec((kh, d), lambda i: (1, 0)),
            pl.BlockSpec((tm, d), lambda i: (r0 + i, 0)),
        ],
        out_specs=pl.BlockSpec((tm, d), lambda i: (r0 + i, 0)),
        out_shape=jax.ShapeDtypeStruct(x.shape, F32),
        input_output_aliases={4: 0},
        compiler_params=_params("parallel"),
        name="out_proj",
    )(a1, a2, w, w, x)


def _rope_table_kernel(pos_ref, inv_ref, cos_ref, sina_ref, sinb_ref):
    ang = pos_ref[...] * inv_ref[...]
    lane = lax.broadcasted_iota(jnp.int32, ang.shape, 1)
    first = (lane & (HD_C - 1)) < HD_C // 2
    sin = jnp.sin(ang)
    cos_ref[...] = jnp.cos(ang)
    sina_ref[...] = jnp.where(first, -sin, 0.0)
    sinb_ref[...] = jnp.where(first, 0.0, sin)


def _rope_table(pos, inv):
    t = pos.shape[0]
    tm = _pick_tile(t, 768, LANES)
    return pl.pallas_call(
        _rope_table_kernel,
        grid=(t // tm,),
        in_specs=[pl.BlockSpec((tm, LANES), lambda i: (i, 0)), pl.BlockSpec((1, LANES), lambda i: (0, 0))],
        out_specs=[pl.BlockSpec((tm, LANES), lambda i: (i, 0))] * 3,
        out_shape=[jax.ShapeDtypeStruct((t, LANES), F32)] * 3,
        compiler_params=_params("parallel"),
        name="rope_table",
    )(pos, inv)


def _proj_odd_kernel(x_ref, g_ref, w_ref, cos_ref, sina_ref, sinb_ref, q_ref, kv_ref, h_ref, *, nq, kwc):
    j = pl.program_id(1)

    @pl.when(j == 0)
    def _():
        h_ref[...] = _rms(x_ref[...], g_ref[...]).astype(BF16)

    y = _dot(h_ref[...], w_ref[...].astype(BF16))
    tn = y.shape[1]
    half = HD_C // 2
    reps = tn // LANES
    roped = (y * _lane_tile(cos_ref[...], reps) + pltpu.roll(y, tn - half, 1) * _lane_tile(sina_ref[...], reps)
             + pltpu.roll(y, half, 1) * _lane_tile(sinb_ref[...], reps))

    @pl.when(j < nq)
    def _():
        q_ref[...] = roped

    @pl.when(j == nq)
    def _():
        kv_ref[:, :kwc] = roped[:, :kwc]
        kv_ref[:, kwc:] = y[:, kwc:]


def _proj_odd(x, row0, n, g, w_all, layer, cos_t, sin_t, qw, kwc):
    d = x.shape[1]
    tn = 2 * kwc
    assert qw % tn == 0 and w_all.shape[2] == qw + tn
    nq = qw // tn
    tm = _pick_tile(n, 1024, LANES)
    assert row0 % tm == 0
    r0 = row0 // tm
    return pl.pallas_call(
        functools.partial(_proj_odd_kernel, nq=nq, kwc=kwc),
        grid=(n // tm, nq + 1),
        in_specs=[
            pl.BlockSpec((tm, d), lambda i, j: (r0 + i, 0)),
            pl.BlockSpec((1, d), lambda i, j: (0, 0)),
            pl.BlockSpec((None, d, tn), lambda i, j: (layer, 0, j)),
        ] + [pl.BlockSpec((tm, LANES), lambda i, j: (r0 + i, 0))] * 3,
        out_specs=[pl.BlockSpec((tm, tn), lambda i, j: (i, jnp.minimum(j, nq - 1))),
                   pl.BlockSpec((tm, tn), lambda i, j: (i, 0))],
        out_shape=[jax.ShapeDtypeStruct((n, qw), F32), jax.ShapeDtypeStruct((n, tn), F32)],
        scratch_shapes=[pltpu.VMEM((tm, d), BF16)],
        compiler_params=_params("parallel", "arbitrary"),
        name="proj_odd",
    )(x, g.reshape(1, d), w_all, cos_t, *sin_t)


def _swa_group(q, kband, vband, mask, sink_col, g, cs):
    tq = q.shape[0]
    q8 = jnp.concatenate([q[:, h * HD_C:(h + 1) * HD_C] for h in range(g)], axis=0).astype(BF16)
    y = _dot_nt(q8, kband.astype(BF16)) * cs
    y = jnp.where(mask, y, NEG_INF)
    m = jnp.maximum(jnp.max(y, axis=1, keepdims=True), sink_col)
    p = jnp.exp2(y - m)
    denom = jnp.sum(p, axis=1, keepdims=True) + jnp.exp2(sink_col - m)
    o = _dot(p.astype(BF16), vband.astype(BF16)) / denom
    return jnp.concatenate([o[h * tq:(h + 1) * tq] for h in range(g)], axis=1)


def _sink_col(sinks_ref, k, g, tq):
    return jnp.concatenate([jnp.full((tq, 1), sinks_ref[k * g + h] * LOG2E, F32) for h in range(g)], axis=0)


def _pair_diag(ref_p, ref_c, col, odd):
    x = jnp.concatenate([ref_p[:, col * LANES:(col + 1) * LANES], ref_c[:, col * LANES:(col + 1) * LANES]], axis=0)
    low = lax.broadcasted_iota(jnp.int32, x.shape, 1) < HD_C
    xr = pltpu.roll(x, HD_C, 1)
    top = jnp.where(low, xr if odd else x, 0.0)
    bot = jnp.where(low, 0.0, x if odd else xr)
    return jnp.concatenate([top, bot], axis=0)


def _swa_prompt_kernel(sinks_ref, q_ref, kvc_ref, kvp_ref, o_ref, *, kv, g, cs):
    i = pl.program_id(1)
    tq = q_ref.shape[0]
    nk = 2 * tq
    npair = g // 2
    cols = npair * tq
    kcols = kv * HD_C // LANES
    kj = lax.broadcasted_iota(jnp.int32, (2 * nk, cols), 0) & (nk - 1)
    qi = lax.broadcasted_iota(jnp.int32, (2 * nk, cols), 1) & (tq - 1)
    bias = jnp.where((kj > qi) & (kj <= qi + tq) & ((kj >= tq) | (i > 0)), 0.0, NEG_INF)
    first = lax.broadcasted_iota(jnp.int32, (LANES, cols), 0) < HD_C
    for k in range(kv):
        kd = _pair_diag(kvp_ref, kvc_ref, k // 2, k % 2).astype(BF16)
        vdt = _pair_diag(kvp_ref, kvc_ref, kcols + k // 2, k % 2).T.astype(BF16)
        q4 = jnp.concatenate([q_ref[:, (k * npair + j) * LANES:(k * npair + j + 1) * LANES] for j in range(npair)],
                             axis=0).astype(BF16)
        y = _dot_nt(kd, q4) * cs + bias
        ps, dens = [], []
        for h in range(2):
            yh = y[h * nk:(h + 1) * nk]
            sink = jnp.concatenate([jnp.full((1, tq), sinks_ref[k * g + 2 * j + h] * LOG2E, F32) for j in range(npair)], axis=1)
            m = jnp.maximum(jnp.max(yh, axis=0, keepdims=True), sink)
            p = jnp.exp2(yh - m)
            ps.append(p.astype(BF16))
            dens.append(jnp.sum(p, axis=0, keepdims=True) + jnp.exp2(sink - m))
        ot = _dot(vdt, jnp.concatenate(ps, axis=0)) / jnp.where(first, dens[0], dens[1])
        o = ot.T
        for j in range(npair):
            c0 = (k * npair + j) * LANES
            o_ref[:, c0:c0 + LANES] = o[j * tq:(j + 1) * tq].astype(o_ref.dtype)


def _swa_prompt(q, kvn, sinks, b, s, kv, g):
    nq = s // WINDOW
    qw = kv * g * HD_C
    kw2 = 2 * kv * HD_C
    return pl.pallas_call(
        functools.partial(_swa_prompt_kernel, kv=kv, g=g, cs=HD_C ** -0.5 * LOG2E),
        grid_spec=pltpu.PrefetchScalarGridSpec(
            num_scalar_prefetch=1,
            grid=(b, nq),
            in_specs=[pl.BlockSpec((WINDOW, qw), lambda bi, i, sk: (bi * nq + i, 0)),
                      pl.BlockSpec((WINDOW, kw2), lambda bi, i, sk: (bi * nq + i, 0)),
                      pl.BlockSpec((WINDOW, kw2), lambda bi, i, sk: (jnp.maximum(bi * nq + i - 1, 0), 0))],
            out_specs=pl.BlockSpec((WINDOW, qw), lambda bi, i, sk: (bi * nq + i, 0)),
        ),
        out_shape=jax.ShapeDtypeStruct((b * s, qw), BF16),
        compiler_params=_params("parallel", "parallel"),
        name="swa_prompt",
    )(sinks, q, kvn, kvn)


def _swa_sample_kernel(sinks_ref, q_ref, kvn_ref, kbuf_ref, vbuf_ref, o_ref, *, kv, g, tt, cs):
    gw = g * HD_C
    kw = kv * HD_C
    nk = 2 * WINDOW
    t_of = lax.broadcasted_iota(jnp.int32, (g * tt, nk), 0) & (tt - 1)
    col = lax.broadcasted_iota(jnp.int32, (g * tt, nk), 1)
    mask = ((col < WINDOW) & (col > t_of)) | ((col >= WINDOW) & (col - WINDOW <= t_of))
    zpad = jnp.zeros((WINDOW - tt, HD_C), F32)
    for k in range(kv):
        ks = slice(k * HD_C, (k + 1) * HD_C)
        vs = slice(kw + k * HD_C, kw + (k + 1) * HD_C)
        kband = jnp.concatenate([kbuf_ref[:, ks], kvn_ref[:, ks], zpad], axis=0)
        vband = jnp.concatenate([vbuf_ref[:, ks], kvn_ref[:, vs], zpad], axis=0)
        o = _swa_group(q_ref[:, k * gw:(k + 1) * gw], kband, vband, mask, _sink_col(sinks_ref, k, g, tt), g, cs)
        o_ref[:, k * gw:(k + 1) * gw] = o


def _swa_sample(q, kvn, sinks, kbuf, vbuf, layer, kv, g, tt):
    db = kbuf.shape[1]
    qw = kv * g * HD_C
    kw = kv * HD_C
    buf_spec = pl.BlockSpec((None, None, WINDOW, kw), lambda b, sk: (layer, b, 0, 0))
    return pl.pallas_call(
        functools.partial(_swa_sample_kernel, kv=kv, g=g, tt=tt, cs=HD_C ** -0.5 * LOG2E),
        grid_spec=pltpu.PrefetchScalarGridSpec(
            num_scalar_prefetch=1,
            grid=(db,),
            in_specs=[pl.BlockSpec((tt, qw), lambda b, sk: (b, 0)),
                      pl.BlockSpec((tt, 2 * kw), lambda b, sk: (b, 0)),
                      buf_spec, buf_spec],
            out_specs=pl.BlockSpec((tt, qw), lambda b, sk: (b, 0)),
        ),
        out_shape=jax.ShapeDtypeStruct((db * tt, qw), F32),
        compiler_params=_params("parallel"),
        name="swa_sample",
    )(sinks, q, kvn, kbuf, vbuf)


def _final_norm_kernel(x_ref, g_ref, o_ref):
    o_ref[...] = _rms(x_ref[...], g_ref[...])


def _final_norm(x, row0, n, g):
    d = x.shape[1]
    tm = _pick_tile(n, 1024, LANES)
    assert row0 % tm == 0
    r0 = row0 // tm
    return pl.pallas_call(
        _final_norm_kernel,
        grid=(n // tm,),
        in_specs=[pl.BlockSpec((tm, d), lambda i: (r0 + i, 0)), pl.BlockSpec((1, d), lambda i: (0, 0))],
        out_specs=pl.BlockSpec((tm, d), lambda i: (i, 0)),
        out_shape=jax.ShapeDtypeStruct((n, d), F32),
        compiler_params=_params("parallel"),
        name="final_norm",
    )(x, g.reshape(1, d))


def kernel(x_prompt, x_sample, cache_sb_k, cache_sb_v, cache_fox_k, cache_fox_v, cache_fox_logf, cache_swa_k, cache_swa_v, page_table, norm_ffn1, norm_mix, norm_ffn2, norm_final, w_ffn_in, w_ffn_out, w_in_even, b_forget, w_out_even, w_in_odd, sinks, w_out_odd):
    b, s, d = x_prompt.shape
    db, tt, _ = x_sample.shape
    depth = norm_mix.shape[0]
    n_even, n_phys, page, kv_a, _ = cache_sb_k.shape
    n_odd = cache_swa_k.shape[0]
    kv_c = cache_swa_k.shape[3]
    h_b = cache_fox_logf.shape[3]
    kv_b = cache_fox_k.shape[3]
    h_a = (w_in_even.shape[2] - h_b - 2 * (kv_a + kv_b) * HEAD_DIM) // HEAD_DIM - h_b
    g_a, g_b = h_a // kv_a, h_b // kv_b
    h_c = w_out_odd.shape[1] // HD_C
    g_c = h_c // kv_c
    assert kv_a == kv_b and g_a == g_b and h_b == SUBLANES and tt == SUBLANES
    bs = b * s
    ns = db * tt
    past_len = page_table.shape[1] * page
    kw = kv_a * HEAD_DIM
    qwa = h_a * HEAD_DIM
    qw = qwa + h_b * HEAD_DIM

    x = jnp.concatenate([x_prompt.reshape(bs, d), x_sample.reshape(ns, d)], axis=0)
    w_even_t = jnp.swapaxes(w_in_even, 1, 2)
    w_in_b = w_ffn_in[0, 0].astype(BF16)
    w_out_b = w_ffn_out[0, 0].astype(BF16)

    paged = [c.reshape(n_even, n_phys, page * kv_a, HEAD_DIM) for c in (cache_sb_k, cache_sb_v, cache_fox_k, cache_fox_v)]
    lf_cache = jnp.swapaxes(cache_fox_logf, 2, 3)
    swa_kbuf = cache_swa_k.reshape(n_odd, db, WINDOW, kv_c * HD_C)
    swa_vbuf = cache_swa_v.reshape(n_odd, db, WINDOW, kv_c * HD_C)

    half = HD_C // 2
    inv_freq = ROPE_THETA ** (-jnp.arange(half, dtype=F32) / half)
    inv = jnp.tile(inv_freq, LANES // half).reshape(1, LANES)
    pos = jnp.concatenate([jnp.tile(jnp.arange(s, dtype=jnp.int32), b),
                           jnp.tile(past_len + jnp.arange(tt, dtype=jnp.int32), db)]).astype(F32)
    cos_t, *sin_t = _rope_table(jnp.broadcast_to(pos[:, None], (bs + ns, LANES)), inv)

    kv_p = [jnp.zeros((n_even, bs, kw), F32) for _ in range(4)]
    kv_s = [jnp.zeros((n_even, ns, kw), F32) for _ in range(4)]
    lf_p, lf_s = [], []
    odd_p = [[], []]
    odd_s = [[], []]

    for l in range(depth):
        x, w_in_b, w_out_b = _ffn(x, norm_ffn1[l], w_in_b, w_out_b, (w_ffn_in, w_ffn_out, l, 1))
        i = l // 2
        if l % 2 == 0:
            w_f = jnp.pad(w_even_t[i, qw + 4 * kw:, :], ((0, LANES - h_b), (0, 0)))
            b_f = jnp.pad(b_forget[i], (0, LANES - h_b)).reshape(1, LANES)
            q_p, kv_p, lfp, lft_p = _proj_even(x, 0, bs, norm_mix[l], w_even_t, w_f, b_f, qwa, qw, kw, i, n_even, kv_p)
            q_s, kv_s, lfs, lft_s = _proj_even(x, bs, ns, norm_mix[l], w_even_t, w_f, b_f, qwa, qw, kw, i, n_even, kv_s)
            lf_p.append(lfp[:, :h_b].reshape(b, s, h_b))
            lf_s.append(lfs[:, :h_b].reshape(db, tt, h_b))
            frow = _fcum(lft_p, b, s).reshape(h_b, 1, bs)
            oa_p = _prompt_attn(q_p, kv_p[0], kv_p[1], None, "sb", i, b, s, kv_a, g_a, 0)
            ob_p = _prompt_attn(q_p, kv_p[2], kv_p[3], frow, "fox", i, b, s, kv_b, g_b, qwa)
            lfn = jnp.pad(lft_s.reshape(h_b, db, tt).transpose(1, 0, 2), ((0, 0), (0, 0), (0, LANES - tt)))
            oa_s, ob_s = _even_sample(q_s, kv_s, lfn, paged, lf_cache, page_table, i, kv_a, g_a, tt)
            w_out = w_out_even[i].astype(BF16)
            x = _out_proj(oa_p, ob_p, 0, w_out, x, 0)
            x = _out_proj(oa_s, ob_s, 0, w_out, x, bs)
        else:
            qwc = h_c * HD_C
            kwc = kv_c * HD_C
            q_p, kvn_p = _proj_odd(x, 0, bs, norm_mix[l], w_in_odd, i, cos_t, sin_t, qwc, kwc)
            q_s, kvn_s = _proj_odd(x, bs, ns, norm_mix[l], w_in_odd, i, cos_t, sin_t, qwc, kwc)
            o_p = _swa_prompt(q_p, kvn_p, sinks[i], b, s, kv_c, g_c)
            o_s = _swa_sample(q_s, kvn_s, sinks[i], swa_kbuf, swa_vbuf, i, kv_c, g_c, tt)
            w_out = w_out_odd[i].astype(BF16)
            x = _out_proj(o_p, o_p, 1, w_out, x, 0)
            x = _out_proj(o_s, o_s, 1, w_out, x, bs)
            for n, buf in enumerate((cache_swa_k[i], cache_swa_v[i])):
                cols = slice(n * kwc, (n + 1) * kwc)
                odd_p[n].append(kvn_p.reshape(b, s, 2 * kwc)[:, s - WINDOW:, cols].reshape(b, WINDOW, kv_c, HD_C))
                odd_s[n].append(jnp.concatenate([buf[:, tt:], kvn_s[:, cols].reshape(db, tt, kv_c, HD_C)], axis=1))
        x, w_in_b, w_out_b = _ffn(x, norm_ffn2[l], w_in_b, w_out_b,
                                  (w_ffn_in, w_ffn_out, l + 1, 0) if l + 1 < depth else None)

    outs = [_final_norm(x, 0, bs, norm_final).reshape(b, s, d), _final_norm(x, bs, ns, norm_final).reshape(db, tt, d)]
    outs += [a.reshape(n_even, b, s, kv_a, HEAD_DIM) for a in kv_p] + [jnp.stack(lf_p)]
    outs += [jnp.stack(a) for a in odd_p]
    outs += [a.reshape(n_even, db, tt, kv_a, HEAD_DIM) for a in kv_s] + [jnp.stack(lf_s)]
    outs += [jnp.stack(a) for a in odd_s]
    return tuple(outs)
```
